```python
import math
import jax, jax.numpy as jnp
from jax import lax
import numpy as np

D_MODEL = 1024
BATCH = 32
SEQ = 256
DEPTH = 4
DEC_BATCH = 8
DEC_SEQ = 2048
PAST_LEN = 512

GRID_W = 64
N_BRANCH = 4
BRANCH_W = D_MODEL // N_BRANCH
HEAD_DIM = 64
N_HEADS = BRANCH_W // HEAD_DIM
RW_DECAY_RANK = 32
RW_A_RANK = 32
RW_GATE_RANK = 64
RW_COLS = 3 * BRANCH_W + RW_DECAY_RANK + RW_A_RANK + RW_GATE_RANK
HY_ORDER = 2
HY_COLS = (HY_ORDER + 1) * BRANCH_W
HY_BANDS = 16
HY_EMB = 2 * HY_BANDS + 1
HY_HID = 64
SHORT_W = 3
RET_COLS = 4 * BRANCH_W
RET_CHUNK = 128
HG_EXPAND = HEAD_DIM
HG_COLS = 5 * BRANCH_W
HG_CHUNK = 32
MG_COLS = N_BRANCH * D_MODEL
N_IN = RW_COLS + HY_COLS + RET_COLS + HG_COLS + MG_COLS
N_EXPERTS = 16
EC_FACTOR = 2
EXPERT_FF = 2 * D_MODEL
ROPE_BASE = 10000.0
NORM_EPS = 1e-6
RW_GN_EPS = 64e-5
GN_EPS = 1e-5
LB_FLOOR = 1e-30

kernel_name = "hybrid_flow_prefix_trunk_step"


def rms_norm(x, g):
    xf = x.astype(jnp.float32)
    y = xf * lax.rsqrt(jnp.mean(xf * xf, axis=-1, keepdims=True) + NORM_EPS)
    return (y * g.astype(jnp.float32)).astype(x.dtype)


def split_heads(t):
    return t.reshape(t.shape[:-1] + (N_HEADS, HEAD_DIM))


def merge_heads(t):
    return t.reshape(t.shape[:-2] + (BRANCH_W,))


def head_layer_norm(y, gain, bias, eps):
    mu = jnp.mean(y, axis=-1, keepdims=True)
    var = jnp.mean(jnp.square(y - mu), axis=-1, keepdims=True)
    return merge_heads((y - mu) * lax.rsqrt(var + eps)) * gain + bias


def centred_taps(x, w_prev, w_mid, w_next):
    xp = jnp.pad(x, ((0, 0), (1, 1), (0, 0)))
    return w_prev * xp[:, :-2] + w_mid * xp[:, 1:-1] + w_next * xp[:, 2:]


def to_dir_heads(fw, bw):
    B, T, _ = fw.shape
    st = jnp.stack([fw, bw[:, ::-1]])
    return st.reshape(2, B, T, N_HEADS, HEAD_DIM).transpose(0, 1, 3, 2, 4)


def bidir(t):
    return to_dir_heads(t, t)


def from_dir_heads(y):
    return jnp.transpose(y[0] + y[1][:, :, ::-1], (0, 2, 1, 3))


def rope_tables(row, col):
    half = HEAD_DIM // 2
    nf = half // 2
    inv = ROPE_BASE ** (-jnp.arange(nf, dtype=jnp.float32) / nf)
    ang = jnp.concatenate([row[:, None] * inv, col[:, None] * inv], axis=-1)
    return jnp.cos(ang)[:, None, :], jnp.sin(ang)[:, None, :]


def apply_rope(x, rope):
    cos, sin = rope
    half = HEAD_DIM // 2
    x1, x2 = x[..., :half], x[..., half:]
    return jnp.concatenate([x1 * cos - x2 * sin, x1 * sin + x2 * cos], axis=-1)


def rwkv7_scan(r, w, kk, kka, v, k, s0):
    def step(S, inp):
        r_t, w_t, kk_t, kka_t, v_t, k_t = inp
        S = (S * w_t[..., None, :]
             - jnp.einsum('dbhvk,dbhk->dbhv', S, kk_t)[..., None] * kka_t[..., None, :]
             + v_t[..., None] * k_t[..., None, :])
        return S, jnp.einsum('dbhvk,dbhk->dbhv', S, r_t)
    xs = tuple(jnp.moveaxis(a, 3, 0) for a in (r, w, kk, kka, v, k))
    S, ys = lax.scan(step, s0, xs)
    return jnp.moveaxis(ys, 0, 3), S


def retention_chunks(q, k, v, log_gamma, s0):
    d, B, H, T, N = q.shape
    C = RET_CHUNK
    nc = T // C

    def chunked(a):
        return jnp.moveaxis(a.reshape(d, B, H, nc, C, a.shape[-1]), 3, 0)

    n = jnp.arange(C, dtype=jnp.float32)
    lg = log_gamma[:, None, :, None, None]
    rel = n[:, None] - n[None, :]
    dmask = jnp.exp(jnp.where(rel >= 0, rel * lg, -jnp.inf))
    q_dec = jnp.exp((n[:, None] + 1.0) * lg)
    k_dec = jnp.exp((C - 1.0 - n[:, None]) * lg)
    c_dec = jnp.exp(C * lg)

    def step(S, inp):
        qc, kc, vc = inp
        att = jnp.einsum('dbhnk,dbhmk->dbhnm', qc, kc) * dmask
        y = (jnp.einsum('dbhnm,dbhmv->dbhnv', att, vc)
             + jnp.einsum('dbhnk,dbhkv->dbhnv', qc * q_dec, S))
        S = S * c_dec + jnp.einsum('dbhmk,dbhmv->dbhkv', kc * k_dec, vc)
        return S, y
    S, ys = lax.scan(step, s0, (chunked(q), chunked(k), chunked(v)))
    return jnp.moveaxis(ys, 0, 3).reshape(d, B, H, T, N), S


def hgrn2_chunks(q, k, log_f, v, s0):
    d, B, H, T, K = q.shape
    V = v.shape[-1]
    C = HG_CHUNK
    nc = T // C

    def chunked(a):
        return jnp.moveaxis(a.reshape(d, B, H, nc, C, a.shape[-1]), 3, 0)

    causal = jnp.tril(jnp.ones((C, C), dtype=bool))[..., None]

    def step(S, inp):
        qc, kc, gc, vc = inp
        b = jnp.cumsum(gc, axis=-2)
        diff = jnp.where(causal, b[..., :, None, :] - b[..., None, :, :], -jnp.inf)
        att = jnp.einsum('dbhtk,dbhsk,dbhtsk->dbhts', qc, kc, jnp.exp(diff))
        y = (jnp.einsum('dbhts,dbhsv->dbhtv', att, vc)
             + jnp.einsum('dbhtk,dbhkv->dbhtv', qc * jnp.exp(b), S))
        b_end = b[..., -1:, :]
        S = (jnp.exp(b_end)[..., 0, :, None] * S
             + jnp.einsum('dbhsk,dbhsv->dbhkv', kc * jnp.exp(b_end - b), vc))
        return S, y
    S, ys = lax.scan(step, s0, (chunked(q), chunked(k), chunked(log_f), chunked(v)))
    return jnp.moveaxis(ys, 0, 3).reshape(d, B, H, T, V), S


def hyena_filter_spectrum(T, lp):
    f32 = jnp.float32
    t = jnp.linspace(0.0, 1.0, T, dtype=f32)[:, None]
    w = (2.0 * math.pi / T) * jnp.arange(T, dtype=f32)[:, None]
    bands = jnp.linspace(1e-4, HY_BANDS - 1.0, HY_BANDS, dtype=f32)[None, :]
    feats = jnp.concatenate([t, jnp.cos(bands * w), -jnp.sin(bands * w)], axis=-1)
    freq = lp['hy_freq'].astype(f32)
    hid = jnp.sin(freq[0] * (feats @ lp['hy_ffn1'].astype(f32) + lp['hy_ffn1_b'].astype(f32)))
    hid = jnp.sin(freq[1] * (hid @ lp['hy_ffn2'].astype(f32) + lp['hy_ffn2_b'].astype(f32)))
    filt = (hid @ lp['hy_ffn3'].astype(f32)) * jnp.exp(-t * jnp.abs(lp['hy_decay'].astype(f32)))
    filt = filt.reshape(T, HY_ORDER, 2, BRANCH_W)
    filt = filt / jnp.sum(jnp.abs(filt), axis=(0, 2), keepdims=True)
    circ = jnp.concatenate([filt[:, :, 0], jnp.zeros((1, HY_ORDER, BRANCH_W), f32),
                            filt[:0:-1, :, 1]], axis=0)
    return jnp.fft.rfft(circ, axis=0)


def fft_long_conv(u, spec):
    T = u.shape[1]
    return jnp.fft.irfft(jnp.fft.rfft(u, n=2 * T, axis=1) * spec, n=2 * T, axis=1)[:, :T]


def token_mix(h, lp, s_rw, s_ret, s_hg, rope):
    B, T, _ = h.shape
    f32 = jnp.float32
    BW = BRANCH_W
    z = jnp.einsum('btd,dn->btn', h, lp['w_in'])
    o1 = RW_COLS
    o2 = o1 + HY_COLS
    o3 = o2 + RET_COLS
    o4 = o3 + HG_COLS
    z_rw, z_hy, z_ret, z_hg, z_mg = jnp.split(z, [o1, o2, o3, o4], axis=-1)

    mu = lp['rw_mu']
    xr = centred_taps(z_rw, mu[0], 1.0 - mu[0] - mu[1], mu[1]).astype(f32)
    r, k, v, xw, xa, xg = jnp.split(
        xr, [BW, 2 * BW, 3 * BW, 3 * BW + RW_DECAY_RANK, 3 * BW + RW_DECAY_RANK + RW_A_RANK], axis=-1)
    wlog = -jax.nn.softplus(-(lp['rw_w0'][:, None, None, :]
                              + jnp.einsum('btr,drc->dbtc', jnp.tanh(xw), lp['rw_w_up']))) - 0.5
    decay = jnp.exp(-jnp.exp(wlog))
    a = jax.nn.sigmoid(lp['rw_a0'] + xa @ lp['rw_a_up'])
    g_rw = jax.nn.sigmoid(xg) @ lp['rw_g_up']
    k_k, k_a, r_k = lp['rw_kvec'][0], lp['rw_kvec'][1], lp['rw_kvec'][2]
    kk = split_heads(k * k_k)
    kk = merge_heads(kk / jnp.maximum(jnp.linalg.norm(kk, axis=-1, keepdims=True), 1e-12))
    k = k * (1.0 + (a - 1.0) * k_a)
    y_dir, s_rw_new = rwkv7_scan(bidir(r), to_dir_heads(decay[0], decay[1]), bidir(kk),
                                 bidir(kk * a), bidir(v), bidir(k), s_rw.astype(f32))
    bonus = jnp.sum(split_heads(r * k * r_k), axis=-1, keepdims=True) * split_heads(v)
    y_rw = (head_layer_norm(from_dir_heads(y_dir), lp['rw_ln'][0], lp['rw_ln'][1], RW_GN_EPS)
            + merge_heads(bonus)) * g_rw

    hc = lp['hy_conv']
    hx = centred_taps(z_hy, hc[0], hc[1], hc[2]).astype(f32)
    u, x1, x2 = jnp.split(hx, 3, axis=-1)
    spec = hyena_filter_spectrum(T, lp)
    skip = lp['hy_skip'].astype(f32)
    u = x1 * (fft_long_conv(u, spec[:, 0]) + skip[0] * u)
    y_hy = x2 * (fft_long_conv(u, spec[:, 1]) + skip[1] * u)

    q, kr, vr, gr = jnp.split(z_ret.astype(f32), 4, axis=-1)
    if rope is not None:
        q = merge_heads(apply_rope(split_heads(q), rope))
        kr = merge_heads(apply_rope(split_heads(kr), rope))
    log_gamma = -jnp.exp(lp['ret_rate'].astype(f32))
    y_dir, s_ret_new = retention_chunks(bidir(q * HEAD_DIM ** -0.5), bidir(kr), bidir(vr),
                                        log_gamma, s_ret.astype(f32))
    y_ret = head_layer_norm(from_dir_heads(y_dir), lp['ret_gn'][0], lp['ret_gn'][1], GN_EPS) * jax.nn.silu(gr)

    qh, zf_fw, zf_bw, ih, gh = jnp.split(z_hg.astype(f32), 5, axis=-1)
    lb = lp['hg_lb'][:, None, None, :]
    zf = jnp.stack([zf_fw, zf_bw])
    log_f = jnp.logaddexp(jnp.log(jnp.maximum(lb, LB_FLOOR)), jnp.log1p(-lb) + jax.nn.log_sigmoid(zf))
    k_in = (1.0 - lb) * jax.nn.sigmoid(-zf)
    y_dir, s_hg_new = hgrn2_chunks(bidir(jax.nn.silu(qh)), to_dir_heads(k_in[0], k_in[1]),
                                   to_dir_heads(log_f[0], log_f[1]), bidir(ih), s_hg.astype(f32))
    yh = from_dir_heads(y_dir)
    yh = yh * lax.rsqrt(jnp.mean(yh * yh, axis=-1, keepdims=True) + NORM_EPS)
    y_hg = merge_heads(yh) * lp['hg_norm'] * jax.nn.silu(gh)

    br = jnp.stack([y_rw, y_hy, y_ret, y_hg], axis=2).astype(h.dtype)
    proj = jnp.einsum('btnc,ncd->btnd', br, lp['br_w'])
    gates = jax.nn.sigmoid(z_mg.reshape(B, T, N_BRANCH, D_MODEL))
    out = jnp.einsum('btd,de->bte', jnp.sum(gates * proj, axis=2), lp['w_out'])
    return out, (s_rw_new, s_ret_new, s_hg_new)


def ec_moe(h, router, w1, w3, w2):
    B, T, D = h.shape
    cap = EC_FACTOR * T // N_EXPERTS
    aff = jax.nn.softmax(jnp.einsum('btd,de->bte', h, router).astype(jnp.float32), axis=-1)
    gate, idx = lax.top_k(jnp.swapaxes(aff, 1, 2), cap)
    xe = jax.vmap(lambda hb, ib: hb[ib])(h, idx)
    hid = jax.nn.silu(jnp.einsum('becd,edf->becf', xe, w1)) * jnp.einsum('becd,edf->becf', xe, w3)
    ye = jnp.einsum('becf,efd->becd', hid, w2) * gate[..., None].astype(h.dtype)
    return jax.vmap(lambda ib, yb: jnp.zeros((T, D), yb.dtype).at[ib.reshape(-1)].add(yb.reshape(-1, D)))(idx, ye)


def trunk_layer(x, cvec, lp, s_rw, s_ret, s_hg, rope):
    mod = jnp.einsum('bd,dm->bm', jax.nn.silu(cvec), lp['ada_w']) + lp['ada_b']
    sh1, sc1, g1, sh2, sc2, g2 = jnp.split(mod[:, None, :], 6, axis=-1)
    h = rms_norm(x, lp['norm1_g']) * (1.0 + sc1) + sh1
    mix, states = token_mix(h, lp, s_rw, s_ret, s_hg, rope)
    x = x + g1 * mix
    h = rms_norm(x, lp['norm2_g']) * (1.0 + sc2) + sh2
    x = x + g2 * ec_moe(h, lp['router'], lp['ex_w1'], lp['ex_w3'], lp['ex_w2'])
    return x, states


def pack_states(states):
    return jnp.stack(states, axis=0).transpose(2, 0, 1, 3, 4, 5)


def setup_inputs(seed: int = 0) -> dict:
    key = jax.random.key(seed)
    keys = jax.random.split(key, 48)
    counter = [0]
    f32 = jnp.float32
    L, D, BW, H, N = DEPTH, D_MODEL, BRANCH_W, N_HEADS, HEAD_DIM

    def nk():
        kk = keys[counter[0]]
        counter[0] += 1
        return kk

    def nrm(shape, scale=1.0):
        return scale * jax.random.normal(nk(), shape, f32)

    inp = {}
    inp['x_prompt'] = nrm((BATCH, SEQ, D))
    inp['x_sample'] = nrm((DEC_BATCH, DEC_SEQ, D))
    inp['state_rwkv'] = nrm((DEC_BATCH, L, 2, H, N, N), 0.5)
    inp['state_ret'] = nrm((DEC_BATCH, L, 2, H, N, N), 0.5)
    inp['state_hgrn'] = nrm((DEC_BATCH, L, 2, H, HG_EXPAND, N), 0.5)
    inp['c'] = nrm((DEC_BATCH, D))
    inp['c_ctx'] = nrm((D,))
    inp['ada_w'] = nrm((L, D, 6 * D), 0.5 * D ** -0.5)
    inp['ada_b'] = nrm((L, 6 * D), 0.02)
    inp['norm1_g'] = 1.0 + nrm((L, D), 0.02)
    inp['norm2_g'] = 1.0 + nrm((L, D), 0.02)
    inp['final_g'] = 1.0 + nrm((D,), 0.02)
    inp['w_in'] = nrm((L, D, N_IN), D ** -0.5)
    inp['rw_mu'] = jax.random.uniform(nk(), (L, 2, RW_COLS), f32, 0.0, 0.4)
    inp['rw_w0'] = jnp.linspace(-6.5, -1.5, BW, dtype=f32)[None, None, :] + nrm((L, 2, BW), 0.1)
    inp['rw_w_up'] = nrm((L, 2, RW_DECAY_RANK, BW), 0.1)
    inp['rw_a0'] = nrm((L, BW), 0.1)
    inp['rw_a_up'] = nrm((L, RW_A_RANK, BW), 0.1)
    inp['rw_g_up'] = nrm((L, RW_GATE_RANK, BW), RW_GATE_RANK ** -0.5)
    inp['rw_kvec'] = jnp.array([0.85, 1.0, -0.04], f32)[None, :, None] + nrm((L, 3, BW), 0.02)
    inp['rw_ln'] = jnp.array([1.0, 0.0], f32)[None, :, None] + nrm((L, 2, BW), 0.02)
    inp['hy_conv'] = nrm((L, SHORT_W, HY_COLS), SHORT_W ** -0.5)
    inp['hy_ffn1'] = nrm((L, HY_EMB, HY_HID), HY_EMB ** -0.5)
    inp['hy_ffn1_b'] = nrm((L, HY_HID), 0.1)
    inp['hy_ffn2'] = nrm((L, HY_HID, HY_HID), HY_HID ** -0.5)
    inp['hy_ffn2_b'] = nrm((L, HY_HID), 0.1)
    inp['hy_ffn3'] = nrm((L, HY_HID, HY_ORDER * 2 * BW), HY_HID ** -0.5)
    inp['hy_freq'] = 1.0 + nrm((L, 2, HY_HID), 0.02)
    inp['hy_decay'] = jnp.tile(jnp.linspace(3.07, 15.35, BW, dtype=f32), HY_ORDER * 2)[None, :] + nrm((L, HY_ORDER * 2 * BW), 0.1)
    inp['hy_skip'] = nrm((L, HY_ORDER, BW))
    ret_base = jnp.log(-jnp.log1p(-(2.0 ** (-5.0 - jnp.arange(H, dtype=f32)))))
    inp['ret_rate'] = ret_base[None, None, :] + nrm((L, 2, H), 0.05)
    inp['ret_gn'] = jnp.array([1.0, 0.0], f32)[None, :, None] + nrm((L, 2, BW), 0.02)
    inp['hg_lb'] = nrm((L, 2, BW), 0.5)
    inp['hg_norm'] = 1.0 + nrm((L, BW), 0.02)
    inp['br_w'] = nrm((L, N_BRANCH, BW, D), BW ** -0.5)
    inp['w_out'] = nrm((L, D, D), D ** -0.5)
    inp['router'] = nrm((L, D, N_EXPERTS), D ** -0.5)
    inp['ex_w1'] = nrm((L, N_EXPERTS, D, EXPERT_FF), D ** -0.5)
    inp['ex_w3'] = nrm((L, N_EXPERTS, D, EXPERT_FF), D ** -0.5)
    inp['ex_w2'] = nrm((L, N_EXPERTS, EXPERT_FF, D), EXPERT_FF ** -0.5)
    return inp


def reference(x_prompt, x_sample, state_rwkv, state_ret, state_hgrn, c, c_ctx, ada_w, ada_b,
              norm1_g, norm2_g, final_g, w_in, rw_mu, rw_w0, rw_w_up, rw_a0, rw_a_up, rw_g_up,
              rw_kvec, rw_ln, hy_conv, hy_ffn1, hy_ffn1_b, hy_ffn2, hy_ffn2_b, hy_ffn3, hy_freq,
              hy_decay, hy_skip, ret_rate, ret_gn, hg_lb, hg_norm, br_w, w_out, router,
              ex_w1, ex_w3, ex_w2):
    f32 = jnp.float32
    n_ctx_req = x_prompt.shape[0]
    n_lat = x_sample.shape[1]
    rows = n_lat // GRID_W
    row = jnp.repeat(jnp.arange(rows, dtype=f32), GRID_W)
    col = jnp.tile(jnp.arange(GRID_W, dtype=f32), rows)
    rope = rope_tables(row, col)
    lb_p = jax.nn.softmax(hg_lb.astype(f32), axis=0)
    lower_bounds = jnp.cumsum(lb_p, axis=0) - lb_p[0]
    zero_rw = jnp.zeros((2, n_ctx_req, N_HEADS, HEAD_DIM, HEAD_DIM), f32)
    zero_ret = jnp.zeros((2, n_ctx_req, N_HEADS, HEAD_DIM, HEAD_DIM), f32)
    zero_hg = jnp.zeros((2, n_ctx_req, N_HEADS, HG_EXPAND, HEAD_DIM), f32)
    c_context = c_ctx[None, :]
    xp, xs = x_prompt, x_sample
    rw_states, ret_states, hg_states = [], [], []
    for l in range(DEPTH):
        lp = {'w_in': w_in[l], 'ada_w': ada_w[l], 'ada_b': ada_b[l], 'norm1_g': norm1_g[l],
              'norm2_g': norm2_g[l], 'rw_mu': rw_mu[l], 'rw_w0': rw_w0[l], 'rw_w_up': rw_w_up[l],
              'rw_a0': rw_a0[l], 'rw_a_up': rw_a_up[l], 'rw_g_up': rw_g_up[l], 'rw_kvec': rw_kvec[l],
              'rw_ln': rw_ln[l], 'hy_conv': hy_conv[l], 'hy_ffn1': hy_ffn1[l], 'hy_ffn1_b': hy_ffn1_b[l],
              'hy_ffn2': hy_ffn2[l], 'hy_ffn2_b': hy_ffn2_b[l], 'hy_ffn3': hy_ffn3[l], 'hy_freq': hy_freq[l],
              'hy_decay': hy_decay[l], 'hy_skip': hy_skip[l], 'ret_rate': ret_rate[l], 'ret_gn': ret_gn[l],
              'hg_lb': lower_bounds[l], 'hg_norm': hg_norm[l], 'br_w': br_w[l], 'w_out': w_out[l],
              'router': router[l], 'ex_w1': ex_w1[l], 'ex_w3': ex_w3[l], 'ex_w2': ex_w2[l]}
        xp, (s_rw, s_ret, s_hg) = trunk_layer(xp, c_context, lp, zero_rw, zero_ret, zero_hg, None)
        rw_states.append(s_rw)
        ret_states.append(s_ret)
        hg_states.append(s_hg)
        xs, _ = trunk_layer(xs, c, lp, jnp.swapaxes(state_rwkv[:, l], 0, 1),
                            jnp.swapaxes(state_ret[:, l], 0, 1), jnp.swapaxes(state_hgrn[:, l], 0, 1), rope)
    y_prompt = rms_norm(xp, final_g)
    y_sample = rms_norm(xs, final_g)
    new_state_rwkv = pack_states(rw_states)
    new_state_ret = pack_states(ret_states)
    new_state_hgrn = pack_states(hg_states)
    return (y_prompt, y_sample, new_state_rwkv, new_state_ret, new_state_hgrn)
```

```python
import functools
import math

import jax
import jax.numpy as jnp
from jax import lax
from jax.experimental import pallas as pl
from jax.experimental.pallas import tpu as pltpu

F32 = jnp.float32
BF16 = jnp.bfloat16

D = 1024
BW = 256
NH = 4
HD = 64
TP = 256
TS = 2048
TM = 256
GRID_W = 64
RW_COLS = 3 * BW + 128
HY_COLS = 3 * BW
RET_COLS = 4 * BW
HG_COLS = 5 * BW
MG_COLS = 4 * D
NE = 16
FF = 2 * D
RET_CHUNK = 128
HG_CHUNK = 32
HY_BANDS = 16
ROPE_BASE = 10000.0
NORM_EPS = 1e-6
RW_GN_EPS = 64e-5
GN_EPS = 1e-5
LB_FLOOR = 1e-30
SCAN_LANES = 128
SCAN_TC = 32
FCH = 256
VMEM_LIMIT = 56 * 1024 * 1024


def _cp(sem, vmem=VMEM_LIMIT):
    return pltpu.CompilerParams(dimension_semantics=sem, vmem_limit_bytes=vmem)


def _sigmoid(x):
    return 1.0 / (1.0 + jnp.exp(-x))


def _silu(x):
    return x * _sigmoid(x)


def _softplus(x):
    return jnp.maximum(x, 0.0) + jnp.log(1.0 + jnp.exp(-jnp.abs(x)))


def _split(a):
    hi = a.astype(BF16)
    lo = (a - hi.astype(F32)).astype(BF16)
    return hi, lo


_NN = (((1,), (0,)), ((), ()))
_NT = (((1,), (1,)), ((), ()))
_TN = (((0,), (0,)), ((), ()))


def _mm(a, b, dims=_NN):
    return lax.dot_general(a, b, dims, preferred_element_type=F32)


def _dot3(a, b, dims=_NN):
    ah, al = _split(a)
    bh, bl = _split(b)
    return _mm(ah, bh, dims) + (_mm(ah, bl, dims) + _mm(al, bh, dims))


def _dot2l(a, b16, dims=_NN):
    ah, al = _split(a)
    return _mm(ah, b16, dims) + _mm(al, b16, dims)


def _dot2r(a16, b, dims=_NN):
    bh, bl = _split(b)
    return _mm(a16, bh, dims) + _mm(a16, bl, dims)


def _mm3(th, tl, xh, xl):
    return _mm(th, xh) + (_mm(th, xl) + _mm(tl, xh))


def _drop_ref(kern, idx):
    def wrapped(*refs):
        return kern(*refs[:idx], *refs[idx + 1:])
    return wrapped


def _ada_kernel(cv_ref, w_ref, b_ref, o_ref):
    cv = cv_ref[...]
    o_ref[0] = _dot3(_silu(cv), w_ref[0]) + b_ref[0]


def _ada(cv, ada_w, ada_b):
    L = ada_w.shape[0]
    tn = 1536
    return pl.pallas_call(
        _ada_kernel,
        grid=(L, 6 * D // tn),
        in_specs=[pl.BlockSpec((16, D), lambda l, j: (0, 0)),
                  pl.BlockSpec((1, D, tn), lambda l, j: (l, 0, j)),
                  pl.BlockSpec((1, 1, tn), lambda l, j: (l, 0, j))],
        out_specs=pl.BlockSpec((1, 16, tn), lambda l, j: (l, 0, j)),
        out_shape=jax.ShapeDtypeStruct((L, 16, 6 * D), F32),
        compiler_params=_cp(("parallel", "parallel")),
        name="ada",
    )(cv, ada_w, ada_b.reshape(L, 1, 6 * D))


def _mod_index(nP):
    def idx(i):
        return (jnp.where(i < nP, 8, (i - nP) // (TS // TM)), 0, 0)
    return idx


def _inproj_kernel(x_ref, g_ref, mod_ref, wrw_ref, why_ref, wret_ref, whg_ref,
                   hb_ref, zrw_ref, zhy_ref, zret_ref, zhg_ref):
    x = x_ref[...]
    y = x * lax.rsqrt(jnp.mean(x * x, axis=-1, keepdims=True) + NORM_EPS) * g_ref[...]
    mod = mod_ref[0]
    h = y * (1.0 + mod[:, D:2 * D]) + mod[:, 0:D]
    hb = h.astype(BF16)
    hb_ref[...] = hb
    zrw_ref[...] = _mm(hb, wrw_ref[...])
    zhy_ref[...] = _mm(hb, why_ref[...])
    zret_ref[...] = _mm(hb, wret_ref[...])
    zhg_ref[...] = _mm(hb, whg_ref[...])


def _inproj(x, g, mod3, w_rw, w_hy, w_ret, w_hg, nP):
    M = x.shape[0]
    row = lambda i: (i, 0)
    const = lambda i: (0, 0)
    widths = (RW_COLS, HY_COLS, RET_COLS, HG_COLS)
    return pl.pallas_call(
        _inproj_kernel,
        grid=(M // TM,),
        in_specs=[pl.BlockSpec((TM, D), row),
                  pl.BlockSpec((1, D), const),
                  pl.BlockSpec((1, 1, 6 * D), _mod_index(nP))]
                 + [pl.BlockSpec((D, w), const) for w in widths],
        out_specs=[pl.BlockSpec((TM, D), row)] + [pl.BlockSpec((TM, w), row) for w in widths],
        out_shape=[jax.ShapeDtypeStruct((M, D), BF16)]
                  + [jax.ShapeDtypeStruct((M, w), F32) for w in widths],
        compiler_params=_cp(("parallel",)),
        name="inproj",
    )(x, g, mod3, w_rw, w_hy, w_ret, w_hg)


def _shift_prev(z, halo8, first):
    rolled = pltpu.roll(z, 1, 0)
    halo = jnp.where(first, 0.0, halo8[7:8, :])
    row = lax.broadcasted_iota(jnp.int32, z.shape, 0)
    return jnp.where(row == 0, halo, rolled)


def _shift_next(z, halo8, last):
    n = z.shape[0]
    rolled = pltpu.roll(z, n - 1, 0)
    halo = jnp.where(last, 0.0, halo8[0:1, :])
    row = lax.broadcasted_iota(jnp.int32, z.shape, 0)
    return jnp.where(row == n - 1, halo, rolled)


def _prep_kernel(nP, zrw_ref, zrw_p_ref, zrw_n_ref, zhy_ref, zhy_p_ref, zhy_n_ref,
                 mu_ref, hc_ref, wlr_ref, vec_ref, bd_ref,
                 r_ref, wf_ref, wb_ref, kk_ref, kka_ref, k_ref, v_ref, bonus_ref, g_ref, hx_ref):
    i = pl.program_id(0)
    j = jnp.maximum(i - nP, 0)
    tiles = TS // TM
    first = jnp.logical_or(i < nP, j % tiles == 0)
    last = jnp.logical_or(i < nP, j % tiles == tiles - 1)

    z = zhy_ref[...]
    hc = hc_ref[...]
    hx_ref[...] = (hc[0:1] * _shift_prev(z, zhy_p_ref[...], first) + hc[1:2] * z
                   + hc[2:3] * _shift_next(z, zhy_n_ref[...], last))

    z = zrw_ref[...]
    mu = mu_ref[...]
    xr = (mu[0:1] * _shift_prev(z, zrw_p_ref[...], first) + (1.0 - mu[0:1] - mu[1:2]) * z
          + mu[1:2] * _shift_next(z, zrw_n_ref[...], last))
    r = xr[:, 0:BW]
    k = xr[:, BW:2 * BW]
    v = xr[:, 2 * BW:3 * BW]
    xl = xr[:, 3 * BW:3 * BW + 128]
    lane = lax.broadcasted_iota(jnp.int32, xl.shape, 1)
    f = jnp.where(lane < 32, jnp.tanh(xl), jnp.where(lane < 64, xl, _sigmoid(xl)))
    lr = _dot3(f, wlr_ref[...])
    vec = vec_ref[...]
    bd = bd_ref[...]
    for d, out in ((0, wf_ref), (1, wb_ref)):
        pre = vec[d:d + 1] + lr[:, d * BW:(d + 1) * BW]
        wlog = -_softplus(-pre) - 0.5
        out[...] = jnp.exp(-jnp.exp(wlog))
    a = _sigmoid(vec[2:3] + lr[:, 2 * BW:3 * BW])
    g_ref[...] = lr[:, 3 * BW:4 * BW]
    kkr = k * vec[3:4]
    nrm = jnp.sqrt(_dot2l(kkr * kkr, bd))
    kk = kkr / jnp.maximum(nrm, 1e-12)
    k2 = k * (1.0 + (a - 1.0) * vec[4:5])
    r_ref[...] = r
    kk_ref[...] = kk
    kka_ref[...] = kk * a
    k_ref[...] = k2
    v_ref[...] = v
    bonus_ref[...] = _dot2l(r * k2 * vec[5:6], bd) * v


def _prep(z_rw, z_hy, mu, hc, wlr, vec, bd, nP):
    M = z_rw.shape[0]
    nb8 = M // 8
    r8 = TM // 8
    row = lambda i: (i, 0)
    prv = lambda i: (jnp.maximum(i * r8 - 1, 0), 0)
    nxt = lambda i: (jnp.minimum(i * r8 + r8, nb8 - 1), 0)
    const = lambda i: (0, 0)
    return pl.pallas_call(
        functools.partial(_prep_kernel, nP),
        grid=(M // TM,),
        in_specs=[pl.BlockSpec((TM, RW_COLS), row), pl.BlockSpec((8, RW_COLS), prv),
                  pl.BlockSpec((8, RW_COLS), nxt),
                  pl.BlockSpec((TM, HY_COLS), row), pl.BlockSpec((8, HY_COLS), prv),
                  pl.BlockSpec((8, HY_COLS), nxt),
                  pl.BlockSpec((2, RW_COLS), const), pl.BlockSpec((3, HY_COLS), const),
                  pl.BlockSpec((128, 4 * BW), const), pl.BlockSpec((8, BW), const),
                  pl.BlockSpec((BW, BW), const)],
        out_specs=[pl.BlockSpec((TM, BW), row)] * 9 + [pl.BlockSpec((TM, HY_COLS), row)],
        out_shape=[jax.ShapeDtypeStruct((M, BW), F32)] * 9
                  + [jax.ShapeDtypeStruct((M, HY_COLS), F32)],
        compiler_params=_cp(("parallel",)),
        name="prep",
    )(z_rw, z_rw, z_rw, z_hy, z_hy, z_hy, mu, hc, wlr, vec, bd)


def _scan_kernel(nc, r_ref, w_ref, kk_ref, kka_ref, k_ref, v_ref, s0_ref, y_ref, sT_ref, s_scr):
    c = pl.program_id(1)

    @pl.when(c == 0)
    def _():
        s_scr[...] = s0_ref[...]

    vp = v_ref.shape[1]

    def step(t, carry):
        vt = v_ref[t]
        sa = jnp.zeros((vp, SCAN_LANES), F32)
        for kc in range(HD):
            sa = sa + s_scr[kc] * kk_ref[t, kc:kc + 1, :]
        y = jnp.zeros((vp, SCAN_LANES), F32)
        for kc in range(HD):
            s_new = (s_scr[kc] * w_ref[t, kc:kc + 1, :]
                     + (vt * k_ref[t, kc:kc + 1, :] - sa * kka_ref[t, kc:kc + 1, :]))
            s_scr[kc] = s_new
            y = y + s_new * r_ref[t, kc:kc + 1, :]
        y_ref[t] = y
        return carry

    lax.fori_loop(0, SCAN_TC, step, 0)

    @pl.when(c == nc - 1)
    def _():
        sT_ref[...] = s_scr[...]


def _scan(r, w, kk, kka, k, v, s0):
    T, _, L = r.shape
    vp = v.shape[1]
    nc = T // SCAN_TC
    rows = pl.BlockSpec((SCAN_TC, HD, SCAN_LANES), lambda b, c: (c, 0, b))
    vrows = pl.BlockSpec((SCAN_TC, vp, SCAN_LANES), lambda b, c: (c, 0, b))
    st = pl.BlockSpec((HD, vp, SCAN_LANES), lambda b, c: (0, 0, b))
    return pl.pallas_call(
        functools.partial(_scan_kernel, nc),
        grid=(L // SCAN_LANES, nc),
        in_specs=[rows] * 5 + [vrows, st],
        out_specs=[vrows, st],
        out_shape=[jax.ShapeDtypeStruct((T, vp, L), F32), jax.ShapeDtypeStruct((HD, vp, L), F32)],
        scratch_shapes=[pltpu.VMEM((HD, vp, SCAN_LANES), F32)],
        compiler_params=_cp(("parallel", "arbitrary")),
        name="rwkv_scan",
    )(r, w, kk, kka, k, v, s0)


def _to_scan_rows(a_fw, a_bw, nb, T, G):
    def one(a, rev):
        a = a.reshape(nb, T, NH, HD)
        if rev:
            a = a[:, ::-1]
        return a.transpose(1, 3, 0, 2).reshape(T, HD, nb * NH)
    both = jnp.concatenate([one(a_fw, False), one(a_bw, True)], axis=-1)
    return jnp.tile(both, (1, 1, G))


def _to_scan_v(v, nb, T, G):
    vp = HD // G
    a = v.reshape(nb, T, NH, G, vp)
    fw = a.transpose(1, 4, 3, 0, 2)
    bw = a[:, ::-1].transpose(1, 4, 3, 0, 2)
    return jnp.stack([fw, bw], axis=3).reshape(T, vp, G * 2 * nb * NH)


def _to_scan_state(s, nb, G):
    vp = HD // G
    a = s.reshape(nb, 2, NH, G, vp, HD)
    return a.transpose(5, 4, 3, 1, 0, 2).reshape(HD, vp, G * 2 * nb * NH)


def _from_scan_y(y, nb, T, G):
    vp = HD // G
    a = y.reshape(T, vp, G, 2, nb, NH).transpose(3, 4, 0, 5, 2, 1)
    a = a.reshape(2, nb, T, BW)
    return a[0].reshape(nb * T, BW), a[1][:, ::-1].reshape(nb * T, BW)


def _from_scan_state(s, nb, G):
    vp = HD // G
    a = s.reshape(HD, vp, G, 2, nb, NH).transpose(4, 3, 5, 2, 1, 0)
    return a.reshape(nb, 2, NH, HD, HD)


def _rwkv_scan_group(parts, s0, nb, T):
    r, wf, wb, kk, kka, k, v = parts
    G = max(1, SCAN_LANES // (2 * nb * NH))
    y, sT = _scan(_to_scan_rows(r, r, nb, T, G), _to_scan_rows(wf, wb, nb, T, G),
                  _to_scan_rows(kk, kk, nb, T, G), _to_scan_rows(kka, kka, nb, T, G),
                  _to_scan_rows(k, k, nb, T, G), _to_scan_v(v, nb, T, G),
                  _to_scan_state(s0, nb, G))
    y_fw, y_bw = _from_scan_y(y, nb, T, G)
    return y_fw, y_bw, _from_scan_state(sT, nb, G)


def _dft_tables(T):
    N = 2 * T
    i = jnp.arange(T, dtype=jnp.int32)
    prod = ((2 * i[:, None] + 1) * (2 * i[None, :] + 1)) % (4 * N)
    ang = prod.astype(F32) * (2.0 * math.pi / (4 * N))
    half = (2 * i + 1).astype(F32) * (math.pi / (2 * N))
    ph = jnp.zeros((T, 128), F32).at[:, 0].set(jnp.cos(half)).at[:, 1].set(jnp.sin(half))
    return _split(jnp.cos(ang)) + _split(jnp.sin(ang)) + (ph,)


def _hy_feats(T):
    t = jnp.linspace(0.0, 1.0, T, dtype=F32)[:, None]
    w = (2.0 * math.pi / T) * jnp.arange(T, dtype=F32)[:, None]
    bands = jnp.linspace(1e-4, HY_BANDS - 1.0, HY_BANDS, dtype=F32)[None, :]
    feats = jnp.concatenate([t, jnp.cos(bands * w), -jnp.sin(bands * w)], axis=-1)
    return jnp.pad(feats, ((0, 0), (0, 128 - feats.shape[1])))


def _spec_kernel(T, feats_ref, f1_ref, b1_ref, f2_ref, b2_ref, f3_ref, freq_ref, dec_ref,
                 ch_ref, cl_ref, sh_ref, sl_ref, ph_ref, re_ref, im_ref, fh_scr, fl_scr):
    j = pl.program_id(0)

    @pl.when(j == 0)
    def _():
        feats = feats_ref[...]
        freq = freq_ref[...]
        h1 = jnp.sin(freq[0:1] * (_dot3(feats, f1_ref[...]) + b1_ref[...]))
        h2 = jnp.sin(freq[1:2] * (_dot3(h1, f2_ref[...]) + b2_ref[...]))
        filt = _dot3(h2, f3_ref[...]) * jnp.exp(-feats[:, 0:1] * jnp.abs(dec_ref[...]))
        row = lax.broadcasted_iota(jnp.int32, (T, BW), 0)
        for o in range(2):
            fw = filt[:, o * 2 * BW:o * 2 * BW + BW]
            bw = filt[:, o * 2 * BW + BW:(o + 1) * 2 * BW]
            den = jnp.sum(jnp.abs(fw) + jnp.abs(bw), axis=0, keepdims=True)
            fw = fw / den
            bw0 = jnp.where(row == 0, 0.0, bw / den)
            for col, val in ((o * BW, fw + bw0), (2 * BW + o * BW, fw - bw0)):
                hi, lo = _split(val)
                fh_scr[:, col:col + BW] = hi
                fl_scr[:, col:col + BW] = lo

    fh = fh_scr[...]
    fl = fl_scr[...]
    c = _mm3(ch_ref[...], cl_ref[...], fh, fl)
    s = _mm3(sh_ref[...], sl_ref[...], fh, fl)
    pc = ph_ref[:, 0:1] * (1.0 / T)
    ps = ph_ref[:, 1:2] * (1.0 / T)
    re_ref[...] = pc * c[:, 0:2 * BW] + ps * s[:, 0:2 * BW]
    im_ref[...] = ps * c[:, 2 * BW:4 * BW] - pc * s[:, 2 * BW:4 * BW]


def _hy_spectrum(T, feats, tabs, f1p, b1, f2, b2, f3, freq, dec):
    ch, cl, sh, sl, ph = tabs
    fch = min(FCH, T)
    const = lambda j: (0, 0)
    rowc = pl.BlockSpec((fch, T), lambda j: (j, 0))
    return pl.pallas_call(
        functools.partial(_spec_kernel, T),
        grid=(T // fch,),
        in_specs=[pl.BlockSpec((T, 128), const), pl.BlockSpec((128, 64), const),
                  pl.BlockSpec((1, 64), const), pl.BlockSpec((64, 64), const),
                  pl.BlockSpec((1, 64), const), pl.BlockSpec((64, 4 * BW), const),
                  pl.BlockSpec((2, 64), const), pl.BlockSpec((1, 4 * BW), const),
                  rowc, rowc, rowc, rowc, pl.BlockSpec((fch, 128), lambda j: (j, 0))],
        out_specs=[pl.BlockSpec((fch, 2 * BW), lambda j: (j, 0))] * 2,
        out_shape=[jax.ShapeDtypeStruct((T, 2 * BW), F32)] * 2,
        scratch_shapes=[pltpu.VMEM((T, 4 * BW), BF16), pltpu.VMEM((T, 4 * BW), BF16)],
        compiler_params=_cp(("arbitrary",)),
        name="hy_spectrum",
    )(feats, f1p, b1, f2, b2, f3, freq, dec, ch, cl, sh, sl, ph)


def _conv_kernel(nj, u_ref, x_ref, skip_ref, sre_ref, sim_ref,
                 crh_ref, crl_ref, srh_ref, srl_ref, cch_ref, ccl_ref, sch_ref, scl_ref,
                 *rest):
    o_ref, uh_scr, ul_scr, acc_scr = rest[-4:]
    j = pl.program_id(1)

    @pl.when(j == 0)
    def _():
        hi, lo = _split(u_ref[...])
        uh_scr[...] = hi
        ul_scr[...] = lo
        acc_scr[...] = jnp.zeros_like(acc_scr)

    uh = uh_scr[...]
    ul = ul_scr[...]
    uc = _mm3(crh_ref[...], crl_ref[...], uh, ul)
    us = _mm3(srh_ref[...], srl_ref[...], uh, ul)
    sre = sre_ref[...]
    sim = sim_ref[...]
    yh, yl = _split(uc * sre + us * sim)
    zh, zl = _split(uc * sim - us * sre)
    acc_scr[...] += (_mm3(cch_ref[...], ccl_ref[...], yh, yl)
                     - _mm3(sch_ref[...], scl_ref[...], zh, zl))

    @pl.when(j == nj - 1)
    def _():
        u = u_ref[...]
        o_ref[...] = x_ref[...] * (acc_scr[...] + skip_ref[...] * u)


def _hy_conv(T, nb, rb0, u_arr, cu, hx, cx, skip, order, spec, tabs, prev_out=None):
    ch, cl, sh, sl, _ = tabs
    sre, sim = spec
    fch = min(FCH, T)
    nj = T // fch
    M = hx.shape[0]
    rowc = pl.BlockSpec((fch, T), lambda b, j: (j, 0))
    colc = pl.BlockSpec((T, fch), lambda b, j: (0, j))
    sp = pl.BlockSpec((fch, BW), lambda b, j: (j, order))
    in_specs = [pl.BlockSpec((T, BW), lambda b, j: (rb0 + b, cu)),
                pl.BlockSpec((T, BW), lambda b, j: (rb0 + b, cx)),
                pl.BlockSpec((1, BW), lambda b, j: (0, 0)),
                sp, sp, rowc, rowc, rowc, rowc, colc, colc, colc, colc]
    args = [u_arr, hx, skip[order][None], sre, sim, ch, cl, sh, sl, ch, cl, sh, sl]
    aliases = {}
    if prev_out is not None:
        in_specs.append(pl.BlockSpec(memory_space=pl.ANY))
        args.append(prev_out)
        aliases = {len(args) - 1: 0}
    return pl.pallas_call(
        functools.partial(_conv_kernel, nj),
        grid=(nb, nj),
        in_specs=in_specs,
        out_specs=pl.BlockSpec((T, BW), lambda b, j: (rb0 + b, 0)),
        out_shape=jax.ShapeDtypeStruct((M, BW), F32),
        scratch_shapes=[pltpu.VMEM((T, BW), BF16), pltpu.VMEM((T, BW), BF16),
                        pltpu.VMEM((T, BW), F32)],
        input_output_aliases=aliases,
        compiler_params=_cp(("parallel", "arbitrary")),
        name="hy_conv",
    )(*args)


def _head_ln(y, bd, gain, bias, eps):
    mu = _dot2l(y, bd) * (1.0 / HD)
    yc = y - mu
    var = _dot2l(yc * yc, bd) * (1.0 / HD)
    return yc * lax.rsqrt(var + eps) * gain + bias


def _ret_kernel(T, use_rope, z_ref, cos_ref, sin_ref, rate_ref, gn_ref, s0_ref, bd_ref,
                y_ref, sT_ref, q_scr, k_scr, y_scr, s_scr, dm_scr):
    C = RET_CHUNK
    nc = T // C
    q = z_ref[:, 0:BW]
    k = z_ref[:, BW:2 * BW]
    if use_rope:
        lane = lax.broadcasted_iota(jnp.int32, (T, BW), 1) % HD
        cos = cos_ref[...]
        sin = sin_ref[...]

        def rope(x):
            sw = jnp.where(lane < HD // 2, pltpu.roll(x, BW - HD // 2, 1), pltpu.roll(x, HD // 2, 1))
            return x * cos + sw * sin
        q = rope(q)
        k = rope(k)
    q_scr[...] = q * (HD ** -0.5)
    k_scr[...] = k
    s_scr[...] = s0_ref[0]

    lg = -jnp.exp(rate_ref[...])
    n_col = lax.broadcasted_iota(jnp.int32, (C, BW), 0).astype(F32)
    n_r = lax.broadcasted_iota(jnp.int32, (C, C), 0)
    n_c = lax.broadcasted_iota(jnp.int32, (C, C), 1)
    qdec = (jnp.exp((n_col + 1.0) * lg[0:1]), jnp.exp((C - n_col) * lg[1:2]))
    kdec = (jnp.exp((C - 1.0 - n_col) * lg[0:1]), jnp.exp(n_col * lg[1:2]))
    cdec = jnp.exp(float(C) * lg)
    for d in range(2):
        rel = (n_r - n_c) if d == 0 else (n_c - n_r)
        relf = jnp.maximum(rel, 0).astype(F32)
        for h in range(NH):
            lgh = lg[d:d + 1, h * HD:h * HD + 1]
            dm_scr[d * NH + h] = jnp.where(rel >= 0, jnp.exp(relf * lgh), 0.0)

    def chunk(d, ci):
        rows = pl.ds(pl.multiple_of(ci * C, C), C)
        qc_all = q_scr[rows, :]
        kc_all = k_scr[rows, :]
        qd_all = qc_all * qdec[d]
        kd_all = kc_all * kdec[d]
        for h in range(NH):
            hs = slice(h * HD, (h + 1) * HD)
            qc = qc_all[:, hs]
            kc = kc_all[:, hs]
            vc = z_ref[rows, 2 * BW + h * HD:2 * BW + (h + 1) * HD]
            s = s_scr[d, h]
            att = _dot3(qc, kc, _NT) * dm_scr[d * NH + h]
            y = _dot3(att, vc) + _dot3(qd_all[:, hs], s)
            s_scr[d, h] = s * cdec[d:d + 1, h * HD:h * HD + 1] + _dot3(kd_all[:, hs], vc, _TN)
            if d == 0:
                y_scr[rows, hs] = y
            else:
                y_scr[rows, hs] = y_scr[rows, hs] + y

    def fw(ci, carry):
        chunk(0, ci)
        return carry

    def bw(i, carry):
        chunk(1, nc - 1 - i)
        return carry

    lax.fori_loop(0, nc, fw, 0)
    lax.fori_loop(0, nc, bw, 0)

    gn = gn_ref[...]
    y = _head_ln(y_scr[...], bd_ref[...], gn[0:1], gn[1:2], GN_EPS)
    y_ref[...] = y * _silu(z_ref[:, 3 * BW:4 * BW])
    sT_ref[0] = s_scr[...]


def _retention(T, nb, rb0, z_ret, cosT, sinT, rate_l, gn, s0, bd, use_rope, prev_out=None):
    M = z_ret.shape[0]
    const = lambda b: (0, 0)
    in_specs = [pl.BlockSpec((T, RET_COLS), lambda b: (rb0 + b, 0)),
                pl.BlockSpec((T, BW), const), pl.BlockSpec((T, BW), const),
                pl.BlockSpec((2, BW), const), pl.BlockSpec((2, BW), const),
                pl.BlockSpec((1, 2, NH, HD, HD), lambda b: (b, 0, 0, 0, 0)),
                pl.BlockSpec((BW, BW), const)]
    args = [z_ret, cosT, sinT, rate_l, gn, s0, bd]
    aliases = {}
    if prev_out is not None:
        in_specs.append(pl.BlockSpec(memory_space=pl.ANY))
        args.append(prev_out)
        aliases = {len(args) - 1: 0}
    kern = functools.partial(_ret_kernel, T, use_rope)
    if prev_out is not None:
        kern = _drop_ref(kern, len(args) - 1)
    return pl.pallas_call(
        kern,
        grid=(nb,),
        in_specs=in_specs,
        out_specs=[pl.BlockSpec((T, BW), lambda b: (rb0 + b, 0)),
                   pl.BlockSpec((1, 2, NH, HD, HD), lambda b: (b, 0, 0, 0, 0))],
        out_shape=[jax.ShapeDtypeStruct((M, BW), F32),
                   jax.ShapeDtypeStruct((nb, 2, NH, HD, HD), F32)],
        scratch_shapes=[pltpu.VMEM((T, BW), F32), pltpu.VMEM((T, BW), F32), pltpu.VMEM((T, BW), F32),
                        pltpu.VMEM((2, NH, HD, HD), F32), pltpu.VMEM((2 * NH, RET_CHUNK, RET_CHUNK), F32)],
        input_output_aliases=aliases,
        compiler_params=_cp(("parallel",)),
        name="retention",
    )(*args)


def _hg_kernel(T, z_ref, lb_ref, norm_ref, s0_ref, bd_ref, tri_ref,
               y_ref, sT_ref, q_scr, kin_scr, lf_scr, y_scr, b_scr, p_scr, st_scr):
    C = HG_CHUNK
    nc = T // C
    q_scr[...] = _silu(z_ref[:, 0:BW])
    for d in range(2):
        zf = z_ref[:, (1 + d) * BW:(2 + d) * BW]
        lb = lb_ref[d:d + 1, :]
        la = jnp.log(jnp.maximum(lb, LB_FLOOR)) + jnp.zeros_like(zf)
        lc = jnp.log(1.0 - lb) - _softplus(-zf)
        lf_scr[d] = jnp.maximum(la, lc) + jnp.log(1.0 + jnp.exp(-jnp.abs(la - lc)))
        kin_scr[d] = (1.0 - lb) * _sigmoid(-zf)
    st_scr[...] = s0_ref[0]
    bd = bd_ref[...]
    s_iota = lax.broadcasted_iota(jnp.int32, (C, BW), 0)

    def chunk(d, ci):
        base = pl.multiple_of(ci * C, C)
        rows = pl.ds(base, C)
        g = lf_scr[d, rows, :]
        b = _dot2r(tri_ref[d], g)
        b_scr[...] = b
        kin = kin_scr[d, rows, :]
        vc = z_ref[rows, 3 * BW:4 * BW]
        p_scr[...] = kin
        b_end = b[C - 1:C, :] if d == 0 else b[0:1, :]

        def row(t, carry):
            bt = b_scr[pl.ds(t, 1), :]
            qt = q_scr[pl.ds(base + t, 1), :]
            mask = (s_iota <= t) if d == 0 else (s_iota >= t)
            e = jnp.where(mask, jnp.exp(jnp.minimum(bt - b_scr[...], 0.0)), 0.0)
            att = _dot2l(e * p_scr[...] * qt, bd)
            yrow = jnp.sum(att * z_ref[rows, 3 * BW:4 * BW], axis=0, keepdims=True)
            if d == 0:
                y_scr[pl.ds(base + t, 1), :] = yrow
            else:
                y_scr[pl.ds(base + t, 1), :] = y_scr[pl.ds(base + t, 1), :] + yrow
            return carry

        lax.fori_loop(0, C, row, 0)

        qe = q_scr[rows, :] * jnp.exp(b)
        kd = kin * jnp.exp(b_end - b)
        eb = jnp.exp(b_end)
        for h in range(NH):
            hs = slice(h * HD, (h + 1) * HD)
            st = st_scr[d, h]
            y_scr[rows, hs] = y_scr[rows, hs] + _dot3(qe[:, hs], st, _NT)
            st_scr[d, h] = eb[:, hs] * st + _dot3(vc[:, hs], kd[:, hs], _TN)

    def fw(ci, carry):
        chunk(0, ci)
        return carry

    def bw(i, carry):
        chunk(1, nc - 1 - i)
        return carry

    lax.fori_loop(0, nc, fw, 0)
    lax.fori_loop(0, nc, bw, 0)

    y = y_scr[...]
    ms = _dot2l(y * y, bd) * (1.0 / HD)
    y_ref[...] = y * lax.rsqrt(ms + NORM_EPS) * norm_ref[...] * _silu(z_ref[:, 4 * BW:5 * BW])
    sT_ref[0] = st_scr[...]


def _hgrn(T, nb, rb0, z_hg, lb, norm, s0t, bd, tri, prev_out=None):
    M = z_hg.shape[0]
    const = lambda b: (0, 0)
    in_specs = [pl.BlockSpec((T, HG_COLS), lambda b: (rb0 + b, 0)),
                pl.BlockSpec((2, BW), const), pl.BlockSpec((1, BW), const),
                pl.BlockSpec((1, 2, NH, HD, HD), lambda b: (b, 0, 0, 0, 0)),
                pl.BlockSpec((BW, BW), const),
                pl.BlockSpec((2, HG_CHUNK, HG_CHUNK), lambda b: (0, 0, 0))]
    args = [z_hg, lb, norm, s0t, bd, tri]
    aliases = {}
    if prev_out is not None:
        in_specs.append(pl.BlockSpec(memory_space=pl.ANY))
        args.append(prev_out)
        aliases = {len(args) - 1: 0}
    kern = functools.partial(_hg_kernel, T)
    if prev_out is not None:
        kern = _drop_ref(kern, len(args) - 1)
    return pl.pallas_call(
        kern,
        grid=(nb,),
        in_specs=in_specs,
        out_specs=[pl.BlockSpec((T, BW), lambda b: (rb0 + b, 0)),
                   pl.BlockSpec((1, 2, NH, HD, HD), lambda b: (b, 0, 0, 0, 0))],
        out_shape=[jax.ShapeDtypeStruct((M, BW), F32),
                   jax.ShapeDtypeStruct((nb, 2, NH, HD, HD), F32)],
        scratch_shapes=[pltpu.VMEM((T, BW), F32), pltpu.VMEM((2, T, BW), F32),
                        pltpu.VMEM((2, T, BW), F32), pltpu.VMEM((T, BW), F32),
                        pltpu.VMEM((HG_CHUNK, BW), F32), pltpu.VMEM((HG_CHUNK, BW), F32),
                        pltpu.VMEM((2, NH, HD, HD), F32)],
        input_output_aliases=aliases,
        compiler_params=_cp(("parallel",)),
        name="hgrn2",
    )(*args)


def _merge_kernel(x_ref, hb_ref, yf_ref, yb_ref, bonus_ref, grw_ref, yhy_ref, yret_ref, yhg_ref,
                  wmg_ref, brw_ref, wout_ref, ln_ref, mod_ref, g2_ref, rt_ref, bd_ref,
                  xo_ref, h2_ref, aff_ref):
    bd = bd_ref[...]
    ln = ln_ref[...]
    y_rw = (_head_ln(yf_ref[...] + yb_ref[...], bd, ln[0:1], ln[1:2], RW_GN_EPS)
            + bonus_ref[...]) * grw_ref[...]
    hb = hb_ref[...]
    merged = jnp.zeros((TM, D), F32)
    for n, y in enumerate((y_rw, yhy_ref[...], yret_ref[...], yhg_ref[...])):
        gate = _sigmoid(_mm(hb, wmg_ref[:, n * D:(n + 1) * D]))
        merged = merged + gate * _mm(y.astype(BF16), brw_ref[n])
    out = _mm(merged.astype(BF16), wout_ref[...])
    mod = mod_ref[0]
    x = x_ref[...] + mod[:, 2 * D:3 * D] * out
    xo_ref[...] = x
    y = x * lax.rsqrt(jnp.mean(x * x, axis=-1, keepdims=True) + NORM_EPS) * g2_ref[...]
    h2 = y * (1.0 + mod[:, 4 * D:5 * D]) + mod[:, 3 * D:4 * D]
    h2_ref[...] = h2.astype(BF16)
    logits = _dot3(rt_ref[...], h2, _NT)
    m = jnp.max(logits, axis=0, keepdims=True)
    e = jnp.exp(logits - m)
    aff_ref[...] = e / jnp.sum(e, axis=0, keepdims=True)


def _merge(x, hb, yf, yb, bonus, grw, y_hy, y_ret, y_hg, w_mg, br_w, w_out, ln, mod3, g2, rt, bd, nP):
    M = x.shape[0]
    row = lambda i: (i, 0)
    const = lambda i: (0, 0)
    small = pl.BlockSpec((TM, BW), row)
    return pl.pallas_call(
        _merge_kernel,
        grid=(M // TM,),
        in_specs=[pl.BlockSpec((TM, D), row), pl.BlockSpec((TM, D), row)] + [small] * 7
                 + [pl.BlockSpec((D, MG_COLS), const), pl.BlockSpec((4, BW, D), lambda i: (0, 0, 0)),
                    pl.BlockSpec((D, D), const), pl.BlockSpec((2, BW), const),
                    pl.BlockSpec((1, 1, 6 * D), _mod_index(nP)), pl.BlockSpec((1, D), const),
                    pl.BlockSpec((NE, D), const), pl.BlockSpec((BW, BW), const)],
        out_specs=[pl.BlockSpec((TM, D), row), pl.BlockSpec((TM, D), row),
                   pl.BlockSpec((NE, TM), lambda i: (0, i))],
        out_shape=[jax.ShapeDtypeStruct((M, D), F32), jax.ShapeDtypeStruct((M, D), BF16),
                   jax.ShapeDtypeStruct((NE, M), F32)],
        compiler_params=_cp(("parallel",)),
        name="merge",
    )(x, hb, yf, yb, bonus, grw, y_hy, y_ret, y_hg, w_mg, br_w, w_out, ln, mod3, g2, rt, bd)


def _topk_kernel(cap, aff_ref, ut_ref, *rest):
    rank_ref = rest[-1]
    aff = aff_ref[...]
    bits = pltpu.bitcast(aff, jnp.int32)
    thr = jnp.zeros((NE, 1), jnp.int32)
    for bit in range(30, -1, -1):
        cand = thr | (1 << bit)
        cnt = jnp.sum(jnp.where(bits >= cand, 1.0, 0.0), axis=1, keepdims=True)
        thr = jnp.where(cnt >= cap, cand, thr)
    gt = bits > thr
    eq = bits == thr
    need = cap - jnp.sum(jnp.where(gt, 1.0, 0.0), axis=1, keepdims=True)
    ut = ut_ref[...]
    ceq = _mm(jnp.where(eq, 1.0, 0.0).astype(BF16), ut)
    sel = jnp.logical_or(gt, jnp.logical_and(eq, ceq <= need))
    rank = _mm(jnp.where(sel, 1.0, 0.0).astype(BF16), ut) - 1.0
    rank_ref[...] = jnp.where(sel, rank, -1.0)


def _topk(affT, T, nb, cb0, ut, prev_out=None):
    M = affT.shape[1]
    in_specs = [pl.BlockSpec((NE, T), lambda b: (0, cb0 + b)), pl.BlockSpec((T, T), lambda b: (0, 0))]
    args = [affT, ut]
    aliases = {}
    if prev_out is not None:
        in_specs.append(pl.BlockSpec(memory_space=pl.ANY))
        args.append(prev_out)
        aliases = {2: 0}
    return pl.pallas_call(
        functools.partial(_topk_kernel, float(2 * T // NE)),
        grid=(nb,),
        in_specs=in_specs,
        out_specs=pl.BlockSpec((NE, T), lambda b: (0, cb0 + b)),
        out_shape=jax.ShapeDtypeStruct((NE, M), F32),
        input_output_aliases=aliases,
        compiler_params=_cp(("parallel",)),
        name="topk",
    )(*args)


def _gather_kernel(cap, rank_ref, h_ref, *rest):
    o_ref = rest[-1]
    T = h_ref.shape[0]
    slot = lax.broadcasted_iota(jnp.int32, (cap, T), 0).astype(F32)
    h = h_ref[...]
    for e in range(NE):
        p = jnp.where(rank_ref[e:e + 1, :] == slot, 1.0, 0.0).astype(BF16)
        o_ref[e] = _mm(p, h).astype(BF16)


def _gather(rank, h2b, T, nb, rb0, sb0, n_slots, prev_out=None):
    cap = 2 * T // NE
    in_specs = [pl.BlockSpec((NE, T), lambda b: (0, rb0 + b)), pl.BlockSpec((T, D), lambda b: (rb0 + b, 0))]
    args = [rank, h2b]
    aliases = {}
    if prev_out is not None:
        in_specs.append(pl.BlockSpec(memory_space=pl.ANY))
        args.append(prev_out)
        aliases = {2: 0}
    return pl.pallas_call(
        functools.partial(_gather_kernel, cap),
        grid=(nb,),
        in_specs=in_specs,
        out_specs=pl.BlockSpec((NE, cap, D), lambda b: (0, sb0 + b, 0)),
        out_shape=jax.ShapeDtypeStruct((NE, n_slots, D), BF16),
        input_output_aliases=aliases,
        compiler_params=_cp(("parallel",)),
        name="moe_gather",
    )(*args)


def _ffn_kernel(x_ref, w1_ref, w3_ref, w2_ref, o_ref):
    x = x_ref[0]
    acc = jnp.zeros(o_ref.shape[1:], F32)
    half = FF // 2
    for f in range(2):
        cols = slice(f * half, (f + 1) * half)
        h1 = _mm(x, w1_ref[0, :, cols])
        h3 = _mm(x, w3_ref[0, :, cols])
        acc = acc + _mm((_silu(h1) * h3).astype(BF16), w2_ref[0, cols, :])
    o_ref[0] = acc


def _ffn(xe, w1, w3, w2):
    n_slots = xe.shape[1]
    tm = 512
    while n_slots % tm:
        tm //= 2
    return pl.pallas_call(
        _ffn_kernel,
        grid=(NE, n_slots // tm),
        in_specs=[pl.BlockSpec((1, tm, D), lambda e, m: (e, m, 0)),
                  pl.BlockSpec((1, D, FF), lambda e, m: (e, 0, 0)),
                  pl.BlockSpec((1, D, FF), lambda e, m: (e, 0, 0)),
                  pl.BlockSpec((1, FF, D), lambda e, m: (e, 0, 0))],
        out_specs=pl.BlockSpec((1, tm, D), lambda e, m: (e, m, 0)),
        out_shape=jax.ShapeDtypeStruct((NE, n_slots, D), F32),
        compiler_params=_cp(("parallel", "parallel")),
        name="moe_ffn",
    )(xe, w1, w3, w2)


def _combine_kernel(cap, x_ref, ye_ref, rank_ref, gate_ref, mod_ref, o_ref):
    T = x_ref.shape[0]
    lane = lax.broadcasted_iota(jnp.int32, (T, cap), 1).astype(F32)
    acc = jnp.zeros(x_ref.shape, F32)
    for e in range(NE):
        p = jnp.where(rank_ref[:, e:e + 1] == lane, 1.0, 0.0).astype(BF16)
        acc = acc + gate_ref[:, e:e + 1] * _dot2r(p, ye_ref[e])
    o_ref[...] = x_ref[...] + mod_ref[0] * acc


def _combine(x, ye, rank_tm, gate_tm, mod3, T, nb, rb0, sb0, nP):
    cap = 2 * T // NE
    M = x.shape[0]
    tn = 256
    per = T // TM
    mod_idx = _mod_index(nP)
    return pl.pallas_call(
        functools.partial(_combine_kernel, cap),
        grid=(nb, D // tn),
        in_specs=[pl.BlockSpec((T, tn), lambda b, j: (rb0 + b, j)),
                  pl.BlockSpec((NE, cap, tn), lambda b, j: (0, sb0 + b, j)),
                  pl.BlockSpec((T, NE), lambda b, j: (rb0 + b, 0)),
                  pl.BlockSpec((T, NE), lambda b, j: (rb0 + b, 0)),
                  pl.BlockSpec((1, 1, tn),
                               lambda b, j: (mod_idx((rb0 + b) * per)[0], 0, 5 * (D // tn) + j))],
        out_specs=pl.BlockSpec((T, tn), lambda b, j: (rb0 + b, j)),
        out_shape=jax.ShapeDtypeStruct((M, D), F32),
        input_output_aliases={0: 0},
        compiler_params=_cp(("parallel", "parallel")),
        name="moe_combine",
    )(x, ye, rank_tm, gate_tm, mod3)


def _final_kernel(x_ref, g_ref, o_ref):
    x = x_ref[...]
    o_ref[...] = x * lax.rsqrt(jnp.mean(x * x, axis=-1, keepdims=True) + NORM_EPS) * g_ref[...]


def _final_norm(x, g):
    M = x.shape[0]
    return pl.pallas_call(
        _final_kernel,
        grid=(M // 512,),
        in_specs=[pl.BlockSpec((512, D), lambda i: (i, 0)), pl.BlockSpec((1, D), lambda i: (0, 0))],
        out_specs=pl.BlockSpec((512, D), lambda i: (i, 0)),
        out_shape=jax.ShapeDtypeStruct((M, D), F32),
        compiler_params=_cp(("parallel",)),
        name="final_norm",
    )(x, g)


def _rope_lane_tables(T):
    rows = T // GRID_W
    row = jnp.repeat(jnp.arange(rows, dtype=F32), GRID_W)
    col = jnp.tile(jnp.arange(GRID_W, dtype=F32), rows)
    nf = HD // 4
    inv = ROPE_BASE ** (-jnp.arange(nf, dtype=F32) / nf)
    ang = jnp.concatenate([row[:, None] * inv, col[:, None] * inv], axis=-1)
    cos, sin = jnp.cos(ang), jnp.sin(ang)
    return (jnp.tile(jnp.concatenate([cos, cos], axis=-1), (1, NH)),
            jnp.tile(jnp.concatenate([-sin, sin], axis=-1), (1, NH)))


def kernel(x_prompt, x_sample, state_rwkv, state_ret, state_hgrn, c, c_ctx, ada_w, ada_b, norm1_g, norm2_g, final_g, w_in, rw_mu, rw_w0, rw_w_up, rw_a0, rw_a_up, rw_g_up, rw_kvec, rw_ln, hy_conv, hy_ffn1, hy_ffn1_b, hy_ffn2, hy_ffn2_b, hy_ffn3, hy_freq, hy_decay, hy_skip, ret_rate, ret_gn, hg_lb, hg_norm, br_w, w_out, router, ex_w1, ex_w3, ex_w2):
    nP, nS, L = x_prompt.shape[0], x_sample.shape[0], w_in.shape[0]
    assert x_prompt.shape[1:] == (TP, D) and x_sample.shape[1:] == (TS, D)
    assert nS <= 8 and (nP * TP) % TS == 0
    MP = nP * TP
    M = MP + nS * TS
    sblk = MP // TS
    x = jnp.concatenate([x_prompt.reshape(MP, D), x_sample.reshape(nS * TS, D)], axis=0)

    cv = jnp.zeros((16, D), F32).at[:nS].set(c).at[8].set(c_ctx)
    mod = _ada(cv, ada_w, ada_b)

    lb_p = jax.nn.softmax(hg_lb.astype(F32), axis=0)
    lower_bounds = jnp.cumsum(lb_p, axis=0) - lb_p[0]

    bd = (jnp.arange(BW)[:, None] // HD == jnp.arange(BW)[None, :] // HD).astype(BF16)
    ci = jnp.arange(HG_CHUNK)
    tri = jnp.stack([ci[:, None] >= ci[None, :], ci[:, None] <= ci[None, :]]).astype(BF16)
    ut = {T: (jnp.arange(T)[:, None] <= jnp.arange(T)[None, :]).astype(BF16) for T in (TP, TS)}
    tabs = {T: _dft_tables(T) for T in (TP, TS)}
    feats = {T: _hy_feats(T) for T in (TP, TS)}
    cosT, sinT = _rope_lane_tables(TS)
    ones_p = jnp.ones((TP, BW), F32)
    zero_state = jnp.zeros((nP, 2, NH, HD, HD), F32)
    groups = ((TP, nP, 0, 0), (TS, nS, sblk, nP * (2 * TP // NE) // (2 * TS // NE)))
    n_slots = nP * (2 * TP // NE) + nS * (2 * TS // NE)

    rw_states, ret_states, hg_states = [], [], []
    for l in range(L):
        mod3 = mod[l].reshape(16, 1, 6 * D)
        wl = w_in[l]
        o1 = RW_COLS
        o2 = o1 + HY_COLS
        o3 = o2 + RET_COLS
        o4 = o3 + HG_COLS
        hb, z_rw, z_hy, z_ret, z_hg = _inproj(
            x, norm1_g[l][None], mod3, wl[:, :o1].astype(BF16), wl[:, o1:o2].astype(BF16),
            wl[:, o2:o3].astype(BF16), wl[:, o3:o4].astype(BF16), nP)

        wlr = jnp.zeros((128, 4 * BW), F32)
        wlr = wlr.at[0:32, 0:BW].set(rw_w_up[l, 0]).at[0:32, BW:2 * BW].set(rw_w_up[l, 1])
        wlr = wlr.at[32:64, 2 * BW:3 * BW].set(rw_a_up[l]).at[64:128, 3 * BW:].set(rw_g_up[l])
        vec = jnp.concatenate([rw_w0[l], rw_a0[l][None], rw_kvec[l], jnp.zeros((2, BW), F32)], axis=0)
        parts = _prep(z_rw, z_hy, rw_mu[l], hy_conv[l], wlr, vec, bd, nP)
        r_, wf_, wb_, kk_, kka_, k_, v_, bonus, g_rw, hx = parts

        scan_in = (r_, wf_, wb_, kk_, kka_, k_, v_)
        yf_p, yb_p, s_rw_p = _rwkv_scan_group([a[:MP] for a in scan_in], zero_state, nP, TP)
        yf_s, yb_s, _ = _rwkv_scan_group([a[MP:] for a in scan_in], state_rwkv[:, l], nS, TS)
        y_fw = jnp.concatenate([yf_p, yf_s], axis=0)
        y_bw = jnp.concatenate([yb_p, yb_s], axis=0)
        rw_states.append(s_rw_p)

        f1p = jnp.pad(hy_ffn1[l], ((0, 128 - hy_ffn1.shape[1]), (0, 0)))
        u1 = y_hy = None
        specs = {}
        for T, nb, rb0, _ in groups:
            specs[T] = _hy_spectrum(T, feats[T], tabs[T], f1p, hy_ffn1_b[l][None], hy_ffn2[l],
                                    hy_ffn2_b[l][None], hy_ffn3[l], hy_freq[l], hy_decay[l][None])
            u1 = _hy_conv(T, nb, rb0, hx, 0, hx, 1, hy_skip[l], 0, specs[T], tabs[T], u1)
        for T, nb, rb0, _ in groups:
            y_hy = _hy_conv(T, nb, rb0, u1, 0, hx, 2, hy_skip[l], 1, specs[T], tabs[T], y_hy)

        y_ret, s_ret_p = _retention(TP, nP, 0, z_ret, ones_p, ones_p, ret_rate[l].repeat(HD, axis=-1),
                                    ret_gn[l], zero_state, bd, False)
        y_ret, _ = _retention(TS, nS, sblk, z_ret, cosT, sinT, ret_rate[l].repeat(HD, axis=-1),
                              ret_gn[l], state_ret[:, l], bd, True, y_ret)
        ret_states.append(s_ret_p)
        y_hg, s_hg_p = _hgrn(TP, nP, 0, z_hg, lower_bounds[l], hg_norm[l][None], zero_state, bd, tri)
        y_hg, _ = _hgrn(TS, nS, sblk, z_hg, lower_bounds[l], hg_norm[l][None],
                        jnp.swapaxes(state_hgrn[:, l], -1, -2), bd, tri, y_hg)
        hg_states.append(jnp.swapaxes(s_hg_p, -1, -2))

        x, h2b, affT = _merge(x, hb, y_fw, y_bw, bonus, g_rw, y_hy, y_ret, y_hg,
                              wl[:, o4:].astype(BF16), br_w[l].astype(BF16), w_out[l].astype(BF16),
                              rw_ln[l], mod3, norm2_g[l][None], router[l].T, bd, nP)

        rank = xe = None
        for T, nb, rb0, sb0 in groups:
            rank = _topk(affT, T, nb, rb0, ut[T], rank)
        for T, nb, rb0, sb0 in groups:
            xe = _gather(rank, h2b, T, nb, rb0, sb0, n_slots, xe)
        ye = _ffn(xe, ex_w1[l].astype(BF16), ex_w3[l].astype(BF16), ex_w2[l].astype(BF16))
        rank_tm, gate_tm = rank.T, affT.T
        for T, nb, rb0, sb0 in groups:
            x = _combine(x, ye, rank_tm, gate_tm, mod3, T, nb, rb0, sb0, nP)

    y = _final_norm(x, final_g[None])
    pack = lambda states: jnp.stack(states, axis=1)
    return (y[:MP].reshape(nP, TP, D), y[MP:].reshape(nS, TS, D),
            pack(rw_states), pack(ret_states), pack(hg_states))
```

```python
import functools
import math

import jax
import jax.numpy as jnp
from jax import lax
from jax.experimental import pallas as pl
from jax.experimental.pallas import tpu as pltpu

F32 = jnp.float32
BF16 = jnp.bfloat16

D = 1024
BW = 256
NH = 4
HD = 64
TP = 256
TS = 2048
TM = 256
GRID_W = 64
RW_COLS = 3 * BW + 128
HY_COLS = 3 * BW
RET_COLS = 4 * BW
HG_COLS = 5 * BW
MG_COLS = 4 * D
NE = 16
FF = 2 * D
RET_CHUNK = 128
HG_CHUNK = 16
HG_GROUP = 4
HY_BANDS = 16
ROPE_BASE = 10000.0
NORM_EPS = 1e-6
RW_GN_EPS = 64e-5
GN_EPS = 1e-5
LB_FLOOR = 1e-30
SCAN_LANES = 128
SCAN_TC = 32
SCAN_VB = 32
SCAN_KU = 32
SCAN_ACC = 4
FCH = 256
VMEM_LIMIT = 56 * 1024 * 1024


def _cp(sem, vmem=VMEM_LIMIT):
    return pltpu.CompilerParams(dimension_semantics=sem, vmem_limit_bytes=vmem)


def _sigmoid(x):
    return 1.0 / (1.0 + jnp.exp(-x))


def _silu(x):
    return x * _sigmoid(x)


def _softplus(x):
    return jnp.maximum(x, 0.0) + jnp.log(1.0 + jnp.exp(-jnp.abs(x)))


def _split(a):
    hi = a.astype(BF16)
    lo = (a - hi.astype(F32)).astype(BF16)
    return hi, lo


_NN = (((1,), (0,)), ((), ()))
_NT = (((1,), (1,)), ((), ()))
_TN = (((0,), (0,)), ((), ()))


def _mm(a, b, dims=_NN):
    return lax.dot_general(a, b, dims, preferred_element_type=F32)


def _dot3(a, b, dims=_NN):
    ah, al = _split(a)
    bh, bl = _split(b)
    return _mm(ah, bh, dims) + (_mm(ah, bl, dims) + _mm(al, bh, dims))


def _dot2l(a, b16, dims=_NN):
    ah, al = _split(a)
    return _mm(ah, b16, dims) + _mm(al, b16, dims)


def _dot2r(a16, b, dims=_NN):
    bh, bl = _split(b)
    return _mm(a16, bh, dims) + _mm(a16, bl, dims)


def _mm3(th, tl, xh, xl):
    return _mm(th, xh) + (_mm(th, xl) + _mm(tl, xh))


def _drop_ref(kern, idx):
    def wrapped(*refs):
        return kern(*refs[:idx], *refs[idx + 1:])
    return wrapped


def _ada_kernel(cv_ref, w_ref, b_ref, o_ref):
    cv = cv_ref[...]
    o_ref[0] = _dot3(_silu(cv), w_ref[0]) + b_ref[0]


def _ada(cv, ada_w, ada_b):
    L = ada_w.shape[0]
    tn = 1536
    return pl.pallas_call(
        _ada_kernel,
        grid=(L, 6 * D // tn),
        in_specs=[pl.BlockSpec((16, D), lambda l, j: (0, 0)),
                  pl.BlockSpec((1, D, tn), lambda l, j: (l, 0, j)),
                  pl.BlockSpec((1, 1, tn), lambda l, j: (l, 0, j))],
        out_specs=pl.BlockSpec((1, 16, tn), lambda l, j: (l, 0, j)),
        out_shape=jax.ShapeDtypeStruct((L, 16, 6 * D), F32),
        compiler_params=_cp(("parallel", "parallel")),
        name="ada",
    )(cv, ada_w, ada_b.reshape(L, 1, 6 * D))


def _mod_index(nP):
    def idx(i):
        return (jnp.where(i < nP, 8, (i - nP) // (TS // TM)), 0, 0)
    return idx


def _inproj_kernel(x_ref, g_ref, mod_ref, wrw_ref, why_ref, wret_ref, whg_ref,
                   hb_ref, zrw_ref, zhy_ref, zret_ref, zhg_ref):
    x = x_ref[...]
    y = x * lax.rsqrt(jnp.mean(x * x, axis=-1, keepdims=True) + NORM_EPS) * g_ref[...]
    mod = mod_ref[0]
    h = y * (1.0 + mod[:, D:2 * D]) + mod[:, 0:D]
    hb = h.astype(BF16)
    hb_ref[...] = hb
    zrw_ref[...] = _mm(hb, wrw_ref[...])
    zhy_ref[...] = _mm(hb, why_ref[...])
    zret_ref[...] = _mm(hb, wret_ref[...])
    zhg_ref[...] = _mm(hb, whg_ref[...])


def _inproj(x, g, mod3, w_rw, w_hy, w_ret, w_hg, nP):
    M = x.shape[0]
    row = lambda i: (i, 0)
    const = lambda i: (0, 0)
    widths = (RW_COLS, HY_COLS, RET_COLS, HG_COLS)
    return pl.pallas_call(
        _inproj_kernel,
        grid=(M // TM,),
        in_specs=[pl.BlockSpec((TM, D), row),
                  pl.BlockSpec((1, D), const),
                  pl.BlockSpec((1, 1, 6 * D), _mod_index(nP))]
                 + [pl.BlockSpec((D, w), const) for w in widths],
        out_specs=[pl.BlockSpec((TM, D), row)] + [pl.BlockSpec((TM, w), row) for w in widths],
        out_shape=[jax.ShapeDtypeStruct((M, D), BF16)]
                  + [jax.ShapeDtypeStruct((M, w), F32) for w in widths],
        compiler_params=_cp(("parallel",)),
        name="inproj",
    )(x, g, mod3, w_rw, w_hy, w_ret, w_hg)


def _shift_prev(z, halo8, first):
    rolled = pltpu.roll(z, 1, 0)
    halo = jnp.where(first, 0.0, halo8[7:8, :])
    row = lax.broadcasted_iota(jnp.int32, z.shape, 0)
    return jnp.where(row == 0, halo, rolled)


def _shift_next(z, halo8, last):
    n = z.shape[0]
    rolled = pltpu.roll(z, n - 1, 0)
    halo = jnp.where(last, 0.0, halo8[0:1, :])
    row = lax.broadcasted_iota(jnp.int32, z.shape, 0)
    return jnp.where(row == n - 1, halo, rolled)


def _prep_kernel(nP, zrw_ref, zrw_p_ref, zrw_n_ref, zhy_ref, zhy_p_ref, zhy_n_ref,
                 mu_ref, hc_ref, wlr_ref, vec_ref, bd_ref,
                 r_ref, wf_ref, wb_ref, kk_ref, kka_ref, k_ref, v_ref, bonus_ref, g_ref, hx_ref):
    i = pl.program_id(0)
    j = jnp.maximum(i - nP, 0)
    tiles = TS // TM
    first = jnp.logical_or(i < nP, j % tiles == 0)
    last = jnp.logical_or(i < nP, j % tiles == tiles - 1)

    z = zhy_ref[...]
    hc = hc_ref[...]
    hx_ref[...] = (hc[0:1] * _shift_prev(z, zhy_p_ref[...], first) + hc[1:2] * z
                   + hc[2:3] * _shift_next(z, zhy_n_ref[...], last))

    z = zrw_ref[...]
    mu = mu_ref[...]
    xr = (mu[0:1] * _shift_prev(z, zrw_p_ref[...], first) + (1.0 - mu[0:1] - mu[1:2]) * z
          + mu[1:2] * _shift_next(z, zrw_n_ref[...], last))
    r = xr[:, 0:BW]
    k = xr[:, BW:2 * BW]
    v = xr[:, 2 * BW:3 * BW]
    xl = xr[:, 3 * BW:3 * BW + 128]
    lane = lax.broadcasted_iota(jnp.int32, xl.shape, 1)
    f = jnp.where(lane < 32, jnp.tanh(xl), jnp.where(lane < 64, xl, _sigmoid(xl)))
    lr = _dot3(f, wlr_ref[...])
    vec = vec_ref[...]
    bd = bd_ref[...]
    for d, out in ((0, wf_ref), (1, wb_ref)):
        pre = vec[d:d + 1] + lr[:, d * BW:(d + 1) * BW]
        wlog = -_softplus(-pre) - 0.5
        out[...] = jnp.exp(-jnp.exp(wlog))
    a = _sigmoid(vec[2:3] + lr[:, 2 * BW:3 * BW])
    g_ref[...] = lr[:, 3 * BW:4 * BW]
    kkr = k * vec[3:4]
    nrm = jnp.sqrt(_dot2l(kkr * kkr, bd))
    kk = kkr / jnp.maximum(nrm, 1e-12)
    k2 = k * (1.0 + (a - 1.0) * vec[4:5])
    r_ref[...] = r
    kk_ref[...] = kk
    kka_ref[...] = kk * a
    k_ref[...] = k2
    v_ref[...] = v
    bonus_ref[...] = _dot2l(r * k2 * vec[5:6], bd) * v


def _prep(z_rw, z_hy, mu, hc, wlr, vec, bd, nP):
    M = z_rw.shape[0]
    nb8 = M // 8
    r8 = TM // 8
    row = lambda i: (i, 0)
    prv = lambda i: (jnp.maximum(i * r8 - 1, 0), 0)
    nxt = lambda i: (jnp.minimum(i * r8 + r8, nb8 - 1), 0)
    const = lambda i: (0, 0)
    return pl.pallas_call(
        functools.partial(_prep_kernel, nP),
        grid=(M // TM,),
        in_specs=[pl.BlockSpec((TM, RW_COLS), row), pl.BlockSpec((8, RW_COLS), prv),
                  pl.BlockSpec((8, RW_COLS), nxt),
                  pl.BlockSpec((TM, HY_COLS), row), pl.BlockSpec((8, HY_COLS), prv),
                  pl.BlockSpec((8, HY_COLS), nxt),
                  pl.BlockSpec((2, RW_COLS), const), pl.BlockSpec((3, HY_COLS), const),
                  pl.BlockSpec((128, 4 * BW), const), pl.BlockSpec((8, BW), const),
                  pl.BlockSpec((BW, BW), const)],
        out_specs=[pl.BlockSpec((TM, BW), row)] * 9 + [pl.BlockSpec((TM, HY_COLS), row)],
        out_shape=[jax.ShapeDtypeStruct((M, BW), F32)] * 9
                  + [jax.ShapeDtypeStruct((M, HY_COLS), F32)],
        compiler_params=_cp(("parallel",)),
        name="prep",
    )(z_rw, z_rw, z_rw, z_hy, z_hy, z_hy, mu, hc, wlr, vec, bd)


def _scan_kernel(nc, r_ref, w_ref, kk_ref, kka_ref, k_ref, v_ref, s0_ref, y_ref, sT_ref, s_scr):
    d = pl.program_id(0)
    c = pl.program_id(1)

    @pl.when(c == 0)
    def _():
        s_scr[...] = s0_ref[0]

    vp = v_ref.shape[1]
    vb = min(vp, SCAN_VB)

    def step(i, carry):
        t = i + d * (SCAN_TC - 1 - 2 * i)
        for v0 in range(0, vp, vb):
            vs = slice(v0, v0 + vb)
            vt = v_ref[t, vs, :]

            zero = jnp.zeros((vb, SCAN_LANES), F32)
            acc0 = (zero,) * SCAN_ACC

            def dot_kk(j, acc):
                out = []
                for a in range(SCAN_ACC):
                    kc = j * SCAN_ACC + a
                    out.append(acc[a] + s_scr[kc, vs, :] * kk_ref[t, pl.ds(kc, 1), :])
                return tuple(out)

            sa = sum(lax.fori_loop(0, HD // SCAN_ACC, dot_kk, acc0, unroll=SCAN_KU // SCAN_ACC))

            def update(j, acc):
                out = []
                for a in range(SCAN_ACC):
                    kc = j * SCAN_ACC + a
                    row = pl.ds(kc, 1)
                    s_new = (s_scr[kc, vs, :] * w_ref[0, t, row, :]
                             + (vt * k_ref[t, row, :] - sa * kka_ref[t, row, :]))
                    s_scr[kc, vs, :] = s_new
                    out.append(acc[a] + s_new * r_ref[t, row, :])
                return tuple(out)

            y_ref[0, t, vs, :] = sum(lax.fori_loop(0, HD // SCAN_ACC, update, acc0,
                                                   unroll=SCAN_KU // SCAN_ACC))
        return carry

    lax.fori_loop(0, SCAN_TC, step, 0)

    @pl.when(c == nc - 1)
    def _():
        sT_ref[0] = s_scr[...]


def _scan(r, w, kk, kka, k, v, s0):
    T = r.shape[0]
    vp = v.shape[1]
    nc = T // SCAN_TC
    tb = lambda d, c: c + d * (nc - 1 - 2 * c)
    rows = pl.BlockSpec((SCAN_TC, HD, SCAN_LANES), lambda d, c: (tb(d, c), 0, 0))
    wrows = pl.BlockSpec((1, SCAN_TC, HD, SCAN_LANES), lambda d, c: (d, tb(d, c), 0, 0))
    vrows = pl.BlockSpec((SCAN_TC, vp, SCAN_LANES), lambda d, c: (tb(d, c), 0, 0))
    yrows = pl.BlockSpec((1, SCAN_TC, vp, SCAN_LANES), lambda d, c: (d, tb(d, c), 0, 0))
    st = pl.BlockSpec((1, HD, vp, SCAN_LANES), lambda d, c: (d, 0, 0, 0))
    return pl.pallas_call(
        functools.partial(_scan_kernel, nc),
        grid=(2, nc),
        in_specs=[rows, wrows, rows, rows, rows, vrows, st],
        out_specs=[yrows, st],
        out_shape=[jax.ShapeDtypeStruct((2, T, vp, SCAN_LANES), F32),
                   jax.ShapeDtypeStruct((2, HD, vp, SCAN_LANES), F32)],
        scratch_shapes=[pltpu.VMEM((HD, vp, SCAN_LANES), F32)],
        compiler_params=_cp(("parallel", "arbitrary")),
        name="rwkv_scan",
    )(r, w, kk, kka, k, v, s0)


def _to_scan_rows(a, nb, T, G):
    a = a.reshape(nb, T, NH, HD).transpose(1, 3, 0, 2).reshape(T, HD, nb * NH)
    return jnp.tile(a, (1, 1, G)) if G > 1 else a


def _to_scan_v(v, nb, T, G):
    vp = HD // G
    return v.reshape(nb, T, NH, G, vp).transpose(1, 4, 3, 0, 2).reshape(T, vp, G * nb * NH)


def _to_scan_state(s, nb, G):
    vp = HD // G
    a = s.reshape(nb, 2, NH, G, vp, HD)
    return a.transpose(1, 5, 4, 3, 0, 2).reshape(2, HD, vp, G * nb * NH)


def _from_scan_y(y, nb, T, G):
    vp = HD // G
    a = y.reshape(2, T, vp, G, nb, NH).transpose(0, 4, 1, 5, 3, 2).reshape(2, nb * T, BW)
    return a[0], a[1]


def _from_scan_state(s, nb, G):
    vp = HD // G
    a = s.reshape(2, HD, vp, G, nb, NH).transpose(4, 0, 5, 3, 2, 1)
    return a.reshape(nb, 2, NH, HD, HD)


def _rwkv_scan_group(parts, s0, nb, T):
    r, wf, wb, kk, kka, k, v = parts
    assert SCAN_LANES % (nb * NH) == 0
    G = SCAN_LANES // (nb * NH)
    w = jnp.stack([_to_scan_rows(wf, nb, T, G), _to_scan_rows(wb, nb, T, G)])
    y, sT = _scan(_to_scan_rows(r, nb, T, G), w, _to_scan_rows(kk, nb, T, G),
                  _to_scan_rows(kka, nb, T, G), _to_scan_rows(k, nb, T, G), _to_scan_v(v, nb, T, G),
                  _to_scan_state(s0, nb, G))
    y_fw, y_bw = _from_scan_y(y, nb, T, G)
    return y_fw, y_bw, _from_scan_state(sT, nb, G)


def _dft_tables(T):
    N = 2 * T
    i = jnp.arange(T, dtype=jnp.int32)
    prod = ((2 * i[:, None] + 1) * (2 * i[None, :] + 1)) % (4 * N)
    ang = prod.astype(F32) * (2.0 * math.pi / (4 * N))
    half = (2 * i + 1).astype(F32) * (math.pi / (2 * N))
    ph = jnp.zeros((T, 128), F32).at[:, 0].set(jnp.cos(half)).at[:, 1].set(jnp.sin(half))
    return _split(jnp.cos(ang)) + _split(jnp.sin(ang)) + (ph,)


def _hy_feats(T):
    t = jnp.linspace(0.0, 1.0, T, dtype=F32)[:, None]
    w = (2.0 * math.pi / T) * jnp.arange(T, dtype=F32)[:, None]
    bands = jnp.linspace(1e-4, HY_BANDS - 1.0, HY_BANDS, dtype=F32)[None, :]
    feats = jnp.concatenate([t, jnp.cos(bands * w), -jnp.sin(bands * w)], axis=-1)
    return jnp.pad(feats, ((0, 0), (0, 128 - feats.shape[1])))


def _spec_kernel(T, feats_ref, f1_ref, b1_ref, f2_ref, b2_ref, f3_ref, freq_ref, dec_ref,
                 ch_ref, cl_ref, sh_ref, sl_ref, ph_ref, re_ref, im_ref, fh_scr, fl_scr):
    j = pl.program_id(0)

    @pl.when(j == 0)
    def _():
        feats = feats_ref[...]
        freq = freq_ref[...]
        h1 = jnp.sin(freq[0:1] * (_dot3(feats, f1_ref[...]) + b1_ref[...]))
        h2 = jnp.sin(freq[1:2] * (_dot3(h1, f2_ref[...]) + b2_ref[...]))
        filt = _dot3(h2, f3_ref[...]) * jnp.exp(-feats[:, 0:1] * jnp.abs(dec_ref[...]))
        row = lax.broadcasted_iota(jnp.int32, (T, BW), 0)
        for o in range(2):
            fw = filt[:, o * 2 * BW:o * 2 * BW + BW]
            bw = filt[:, o * 2 * BW + BW:(o + 1) * 2 * BW]
            den = jnp.sum(jnp.abs(fw) + jnp.abs(bw), axis=0, keepdims=True)
            fw = fw / den
            bw0 = jnp.where(row == 0, 0.0, bw / den)
            for col, val in ((o * BW, fw + bw0), (2 * BW + o * BW, fw - bw0)):
                hi, lo = _split(val)
                fh_scr[:, col:col + BW] = hi
                fl_scr[:, col:col + BW] = lo

    fh = fh_scr[...]
    fl = fl_scr[...]
    c = _mm3(ch_ref[...], cl_ref[...], fh, fl)
    s = _mm3(sh_ref[...], sl_ref[...], fh, fl)
    pc = ph_ref[:, 0:1] * (1.0 / T)
    ps = ph_ref[:, 1:2] * (1.0 / T)
    re_ref[...] = pc * c[:, 0:2 * BW] + ps * s[:, 0:2 * BW]
    im_ref[...] = ps * c[:, 2 * BW:4 * BW] - pc * s[:, 2 * BW:4 * BW]


def _hy_spectrum(T, feats, tabs, f1p, b1, f2, b2, f3, freq, dec):
    ch, cl, sh, sl, ph = tabs
    fch = min(FCH, T)
    const = lambda j: (0, 0)
    rowc = pl.BlockSpec((fch, T), lambda j: (j, 0))
    return pl.pallas_call(
        functools.partial(_spec_kernel, T),
        grid=(T // fch,),
        in_specs=[pl.BlockSpec((T, 128), const), pl.BlockSpec((128, 64), const),
                  pl.BlockSpec((1, 64), const), pl.BlockSpec((64, 64), const),
                  pl.BlockSpec((1, 64), const), pl.BlockSpec((64, 4 * BW), const),
                  pl.BlockSpec((2, 64), const), pl.BlockSpec((1, 4 * BW), const),
                  rowc, rowc, rowc, rowc, pl.BlockSpec((fch, 128), lambda j: (j, 0))],
        out_specs=[pl.BlockSpec((fch, 2 * BW), lambda j: (j, 0))] * 2,
        out_shape=[jax.ShapeDtypeStruct((T, 2 * BW), F32)] * 2,
        scratch_shapes=[pltpu.VMEM((T, 4 * BW), BF16), pltpu.VMEM((T, 4 * BW), BF16)],
        compiler_params=_cp(("arbitrary",)),
        name="hy_spectrum",
    )(feats, f1p, b1, f2, b2, f3, freq, dec, ch, cl, sh, sl, ph)


def _conv_kernel(nj, u_ref, x_ref, skip_ref, sre_ref, sim_ref,
                 crh_ref, crl_ref, srh_ref, srl_ref, cch_ref, ccl_ref, sch_ref, scl_ref,
                 *rest):
    o_ref, uh_scr, ul_scr, acc_scr = rest[-4:]
    j = pl.program_id(1)

    @pl.when(j == 0)
    def _():
        hi, lo = _split(u_ref[...])
        uh_scr[...] = hi
        ul_scr[...] = lo
        acc_scr[...] = jnp.zeros_like(acc_scr)

    uh = uh_scr[...]
    ul = ul_scr[...]
    uc = _mm3(crh_ref[...], crl_ref[...], uh, ul)
    us = _mm3(srh_ref[...], srl_ref[...], uh, ul)
    sre = sre_ref[...]
    sim = sim_ref[...]
    yh, yl = _split(uc * sre + us * sim)
    zh, zl = _split(uc * sim - us * sre)
    acc_scr[...] += (_mm3(cch_ref[...], ccl_ref[...], yh, yl)
                     - _mm3(sch_ref[...], scl_ref[...], zh, zl))

    @pl.when(j == nj - 1)
    def _():
        u = u_ref[...]
        o_ref[...] = x_ref[...] * (acc_scr[...] + skip_ref[...] * u)


def _hy_conv(T, nb, rb0, u_arr, cu, hx, cx, skip, order, spec, tabs, prev_out=None):
    ch, cl, sh, sl, _ = tabs
    sre, sim = spec
    fch = min(FCH, T)
    nj = T // fch
    M = hx.shape[0]
    rowc = pl.BlockSpec((fch, T), lambda b, j: (j, 0))
    colc = pl.BlockSpec((T, fch), lambda b, j: (0, j))
    sp = pl.BlockSpec((fch, BW), lambda b, j: (j, order))
    in_specs = [pl.BlockSpec((T, BW), lambda b, j: (rb0 + b, cu)),
                pl.BlockSpec((T, BW), lambda b, j: (rb0 + b, cx)),
                pl.BlockSpec((1, BW), lambda b, j: (0, 0)),
                sp, sp, rowc, rowc, rowc, rowc, colc, colc, colc, colc]
    args = [u_arr, hx, skip[order][None], sre, sim, ch, cl, sh, sl, ch, cl, sh, sl]
    aliases = {}
    if prev_out is not None:
        in_specs.append(pl.BlockSpec(memory_space=pl.ANY))
        args.append(prev_out)
        aliases = {len(args) - 1: 0}
    return pl.pallas_call(
        functools.partial(_conv_kernel, nj),
        grid=(nb, nj),
        in_specs=in_specs,
        out_specs=pl.BlockSpec((T, BW), lambda b, j: (rb0 + b, 0)),
        out_shape=jax.ShapeDtypeStruct((M, BW), F32),
        scratch_shapes=[pltpu.VMEM((T, BW), BF16), pltpu.VMEM((T, BW), BF16),
                        pltpu.VMEM((T, BW), F32)],
        input_output_aliases=aliases,
        compiler_params=_cp(("parallel", "arbitrary")),
        name="hy_conv",
    )(*args)


def _head_ln(y, bd, gain, bias, eps):
    mu = _dot2l(y, bd) * (1.0 / HD)
    yc = y - mu
    var = _dot2l(yc * yc, bd) * (1.0 / HD)
    return yc * lax.rsqrt(var + eps) * gain + bias


def _ret_kernel(T, use_rope, z_ref, cos_ref, sin_ref, rate_ref, gn_ref, s0_ref, bd_ref,
                y_ref, sT_ref, q_scr, k_scr, y_scr, s_scr, dm_scr):
    C = RET_CHUNK
    nc = T // C
    q = z_ref[:, 0:BW]
    k = z_ref[:, BW:2 * BW]
    if use_rope:
        lane = lax.broadcasted_iota(jnp.int32, (T, BW), 1) % HD
        cos = cos_ref[...]
        sin = sin_ref[...]

        def rope(x):
            sw = jnp.where(lane < HD // 2, pltpu.roll(x, BW - HD // 2, 1), pltpu.roll(x, HD // 2, 1))
            return x * cos + sw * sin
        q = rope(q)
        k = rope(k)
    q_scr[...] = q * (HD ** -0.5)
    k_scr[...] = k
    s_scr[...] = s0_ref[0]

    lg = -jnp.exp(rate_ref[...])
    n_col = lax.broadcasted_iota(jnp.int32, (C, BW), 0).astype(F32)
    n_r = lax.broadcasted_iota(jnp.int32, (C, C), 0)
    n_c = lax.broadcasted_iota(jnp.int32, (C, C), 1)
    qdec = (jnp.exp((n_col + 1.0) * lg[0:1]), jnp.exp((C - n_col) * lg[1:2]))
    kdec = (jnp.exp((C - 1.0 - n_col) * lg[0:1]), jnp.exp(n_col * lg[1:2]))
    cdec = jnp.exp(float(C) * lg)
    for d in range(2):
        rel = (n_r - n_c) if d == 0 else (n_c - n_r)
        relf = jnp.maximum(rel, 0).astype(F32)
        for h in range(NH):
            lgh = lg[d:d + 1, h * HD:h * HD + 1]
            dm_scr[d * NH + h] = jnp.where(rel >= 0, jnp.exp(relf * lgh), 0.0)

    def both(i, carry):
        units = []
        for d in range(2):
            rows = pl.ds(pl.multiple_of((i if d == 0 else nc - 1 - i) * C, C), C)
            qc = q_scr[rows, :]
            kc = k_scr[rows, :]
            qb = qc.astype(BF16)
            kb = kc.astype(BF16)
            qd = (qc * qdec[d]).astype(BF16)
            kd = (kc * kdec[d]).astype(BF16)
            vb = z_ref[rows, 2 * BW:3 * BW].astype(BF16)
            for h in range(NH):
                hs = slice(h * HD, (h + 1) * HD)
                s = s_scr[d, h]
                units.append((d, h, rows, vb[:, hs], s,
                              _mm(qb[:, hs], kb[:, hs], _NT),
                              _mm(qd[:, hs], s.astype(BF16)),
                              _mm(kd[:, hs], vb[:, hs], _TN)))
        ys = [[], []]
        for d, h, rows, vh, s, att, inter, upd in units:
            att = (att * dm_scr[d * NH + h]).astype(BF16)
            ys[d].append(_mm(att, vh) + inter)
            s_scr[d, h] = s * cdec[d:d + 1, h * HD:h * HD + 1] + upd
        for d in range(2):
            y_scr[d, units[d * NH][2], :] = jnp.concatenate(ys[d], axis=1)
        return carry

    lax.fori_loop(0, nc, both, 0)

    gn = gn_ref[...]
    y = _head_ln(y_scr[0] + y_scr[1], bd_ref[...], gn[0:1], gn[1:2], GN_EPS)
    y_ref[...] = y * _silu(z_ref[:, 3 * BW:4 * BW])
    sT_ref[0] = s_scr[...]


def _retention(T, nb, rb0, z_ret, cosT, sinT, rate_l, gn, s0, bd, use_rope, prev_out=None):
    M = z_ret.shape[0]
    const = lambda b: (0, 0)
    in_specs = [pl.BlockSpec((T, RET_COLS), lambda b: (rb0 + b, 0)),
                pl.BlockSpec((T, BW), const), pl.BlockSpec((T, BW), const),
                pl.BlockSpec((2, BW), const), pl.BlockSpec((2, BW), const),
                pl.BlockSpec((1, 2, NH, HD, HD), lambda b: (b, 0, 0, 0, 0)),
                pl.BlockSpec((BW, BW), const)]
    args = [z_ret, cosT, sinT, rate_l, gn, s0, bd]
    aliases = {}
    if prev_out is not None:
        in_specs.append(pl.BlockSpec(memory_space=pl.ANY))
        args.append(prev_out)
        aliases = {len(args) - 1: 0}
    kern = functools.partial(_ret_kernel, T, use_rope)
    if prev_out is not None:
        kern = _drop_ref(kern, len(args) - 1)
    return pl.pallas_call(
        kern,
        grid=(nb,),
        in_specs=in_specs,
        out_specs=[pl.BlockSpec((T, BW), lambda b: (rb0 + b, 0)),
                   pl.BlockSpec((1, 2, NH, HD, HD), lambda b: (b, 0, 0, 0, 0))],
        out_shape=[jax.ShapeDtypeStruct((M, BW), F32),
                   jax.ShapeDtypeStruct((nb, 2, NH, HD, HD), F32)],
        scratch_shapes=[pltpu.VMEM((T, BW), F32), pltpu.VMEM((T, BW), F32), pltpu.VMEM((2, T, BW), F32),
                        pltpu.VMEM((2, NH, HD, HD), F32), pltpu.VMEM((2 * NH, RET_CHUNK, RET_CHUNK), F32)],
        input_output_aliases=aliases,
        compiler_params=_cp(("parallel",)),
        name="retention",
    )(*args)


def _hg_kernel(T, z_ref, lb_ref, norm_ref, s0_ref, bd_ref, tri_ref,
               y_ref, sT_ref, q_scr, kin_scr, lf_scr, y_scr, st_scr):
    C = HG_CHUNK
    nc = T // C
    q_scr[...] = _silu(z_ref[:, 0:BW])
    for d in range(2):
        zf = z_ref[:, (1 + d) * BW:(2 + d) * BW]
        lb = lb_ref[d:d + 1, :]
        la = jnp.log(jnp.maximum(lb, LB_FLOOR)) + jnp.zeros_like(zf)
        lc = jnp.log(1.0 - lb) - _softplus(-zf)
        lf_scr[d] = jnp.maximum(la, lc) + jnp.log(1.0 + jnp.exp(-jnp.abs(la - lc)))
        kin_scr[d] = (1.0 - lb) * _sigmoid(-zf)
    st_scr[...] = s0_ref[0]
    bd = bd_ref[...]
    row = lax.broadcasted_iota(jnp.int32, (C, BW), 0)
    G = HG_GROUP
    R = G * C
    ng = T // R

    def shifted(a, o, d):
        if o == 0:
            return a
        return pltpu.roll(a, o if d == 0 else C - o, 0)

    def group(i, carry):
        dirs = []
        for d in range(2):
            rows = pl.ds(pl.multiple_of((i if d == 0 else ng - 1 - i) * R, R), R)
            g = lf_scr[d, rows, :]
            g1 = g.astype(BF16)
            g2 = (g - g1.astype(F32)).astype(BF16)
            g3 = (g - g1.astype(F32) - g2.astype(F32)).astype(BF16)
            tri = tri_ref[d]
            b = _mm(tri, g1) + (_mm(tri, g2) + _mm(tri, g3))
            dirs.append((rows, b, q_scr[rows, :], kin_scr[d, rows, :], z_ref[rows, 3 * BW:4 * BW]))

        atts = []
        for d, (rows, b, q, kin, v) in enumerate(dirs):
            ps = []
            for c in range(G):
                cs = slice(c * C, (c + 1) * C)
                bc, qc, kc = b[cs], q[cs], kin[cs]
                for o in range(C):
                    mask = (row >= o) if d == 0 else (row <= C - 1 - o)
                    e = jnp.exp(jnp.minimum(bc - shifted(bc, o, d), 0.0))
                    ps.append(jnp.where(mask, e * shifted(kc, o, d) * qc, 0.0).astype(BF16))
            atts.append(_mm(jnp.concatenate(ps, axis=0), bd))

        upd = []
        for d, (rows, b, q, kin, v) in enumerate(dirs):
            per_chunk = []
            for c in range(G):
                cs = slice(c * C, (c + 1) * C)
                bc = b[cs]
                b_end = bc[C - 1:C, :] if d == 0 else bc[0:1, :]
                kd = (kin[cs] * jnp.exp(b_end - bc)).astype(BF16)
                vb = v[cs].astype(BF16)
                u = [_mm(vb[:, h * HD:(h + 1) * HD], kd[:, h * HD:(h + 1) * HD], _TN) for h in range(NH)]
                per_chunk.append((u, jnp.exp(b_end), (q[cs] * jnp.exp(bc)).astype(BF16)))
            upd.append(per_chunk)

        for d, (rows, b, q, kin, v) in enumerate(dirs):
            att = atts[d]
            ys = [None] * G
            for c in range(G):
                vc = v[c * C:(c + 1) * C]
                y = jnp.zeros((C, BW), F32)
                for o in range(C):
                    y = y + att[(c * C + o) * C:(c * C + o + 1) * C, :] * shifted(vc, o, d)
                ys[c] = y
            st = [st_scr[d, h] for h in range(NH)]
            for c in (range(G) if d == 0 else range(G - 1, -1, -1)):
                u, eb, qe = upd[d][c]
                inter = []
                for h in range(NH):
                    hs = slice(h * HD, (h + 1) * HD)
                    inter.append(_mm(qe[:, hs], st[h].astype(BF16), _NT))
                    st[h] = eb[:, hs] * st[h] + u[h]
                ys[c] = ys[c] + jnp.concatenate(inter, axis=1)
            for h in range(NH):
                st_scr[d, h] = st[h]
            y_scr[d, rows, :] = jnp.concatenate(ys, axis=0)
        return carry

    lax.fori_loop(0, ng, group, 0)

    y = y_scr[0] + y_scr[1]
    ms = _dot2l(y * y, bd) * (1.0 / HD)
    y_ref[...] = y * lax.rsqrt(ms + NORM_EPS) * norm_ref[...] * _silu(z_ref[:, 4 * BW:5 * BW])
    sT_ref[0] = st_scr[...]


def _hg_tri():
    ci = jnp.arange(HG_CHUNK * HG_GROUP)
    same = ci[:, None] // HG_CHUNK == ci[None, :] // HG_CHUNK
    return jnp.stack([same & (ci[:, None] >= ci[None, :]), same & (ci[:, None] <= ci[None, :])]).astype(BF16)


def _hgrn(T, nb, rb0, z_hg, lb, norm, s0t, bd, tri, prev_out=None):
    M = z_hg.shape[0]
    const = lambda b: (0, 0)
    in_specs = [pl.BlockSpec((T, HG_COLS), lambda b: (rb0 + b, 0)),
                pl.BlockSpec((2, BW), const), pl.BlockSpec((1, BW), const),
                pl.BlockSpec((1, 2, NH, HD, HD), lambda b: (b, 0, 0, 0, 0)),
                pl.BlockSpec((BW, BW), const),
                pl.BlockSpec((2, HG_CHUNK * HG_GROUP, HG_CHUNK * HG_GROUP), lambda b: (0, 0, 0))]
    args = [z_hg, lb, norm, s0t, bd, tri]
    aliases = {}
    if prev_out is not None:
        in_specs.append(pl.BlockSpec(memory_space=pl.ANY))
        args.append(prev_out)
        aliases = {len(args) - 1: 0}
    kern = functools.partial(_hg_kernel, T)
    if prev_out is not None:
        kern = _drop_ref(kern, len(args) - 1)
    return pl.pallas_call(
        kern,
        grid=(nb,),
        in_specs=in_specs,
        out_specs=[pl.BlockSpec((T, BW), lambda b: (rb0 + b, 0)),
                   pl.BlockSpec((1, 2, NH, HD, HD), lambda b: (b, 0, 0, 0, 0))],
        out_shape=[jax.ShapeDtypeStruct((M, BW), F32),
                   jax.ShapeDtypeStruct((nb, 2, NH, HD, HD), F32)],
        scratch_shapes=[pltpu.VMEM((T, BW), F32), pltpu.VMEM((2, T, BW), F32),
                        pltpu.VMEM((2, T, BW), F32), pltpu.VMEM((2, T, BW), F32),
                        pltpu.VMEM((2, NH, HD, HD), F32)],
        input_output_aliases=aliases,
        compiler_params=_cp(("parallel",)),
        name="hgrn2",
    )(*args)


def _merge_kernel(x_ref, hb_ref, yf_ref, yb_ref, bonus_ref, grw_ref, yhy_ref, yret_ref, yhg_ref,
                  wmg_ref, brw_ref, wout_ref, ln_ref, mod_ref, g2_ref, rt_ref, bd_ref,
                  xo_ref, h2_ref, aff_ref):
    bd = bd_ref[...]
    ln = ln_ref[...]
    y_rw = (_head_ln(yf_ref[...] + yb_ref[...], bd, ln[0:1], ln[1:2], RW_GN_EPS)
            + bonus_ref[...]) * grw_ref[...]
    hb = hb_ref[...]
    merged = jnp.zeros((TM, D), F32)
    for n, y in enumerate((y_rw, yhy_ref[...], yret_ref[...], yhg_ref[...])):
        gate = _sigmoid(_mm(hb, wmg_ref[:, n * D:(n + 1) * D]))
        merged = merged + gate * _mm(y.astype(BF16), brw_ref[n])
    out = _mm(merged.astype(BF16), wout_ref[...])
    mod = mod_ref[0]
    x = x_ref[...] + mod[:, 2 * D:3 * D] * out
    xo_ref[...] = x
    y = x * lax.rsqrt(jnp.mean(x * x, axis=-1, keepdims=True) + NORM_EPS) * g2_ref[...]
    h2 = y * (1.0 + mod[:, 4 * D:5 * D]) + mod[:, 3 * D:4 * D]
    h2_ref[...] = h2.astype(BF16)
    logits = _dot3(rt_ref[...], h2, _NT)
    m = jnp.max(logits, axis=0, keepdims=True)
    e = jnp.exp(logits - m)
    aff_ref[...] = e / jnp.sum(e, axis=0, keepdims=True)


def _merge(x, hb, yf, yb, bonus, grw, y_hy, y_ret, y_hg, w_mg, br_w, w_out, ln, mod3, g2, rt, bd, nP):
    M = x.shape[0]
    row = lambda i: (i, 0)
    const = lambda i: (0, 0)
    small = pl.BlockSpec((TM, BW), row)
    return pl.pallas_call(
        _merge_kernel,
        grid=(M // TM,),
        in_specs=[pl.BlockSpec((TM, D), row), pl.BlockSpec((TM, D), row)] + [small] * 7
                 + [pl.BlockSpec((D, MG_COLS), const), pl.BlockSpec((4, BW, D), lambda i: (0, 0, 0)),
                    pl.BlockSpec((D, D), const), pl.BlockSpec((2, BW), const),
                    pl.BlockSpec((1, 1, 6 * D), _mod_index(nP)), pl.BlockSpec((1, D), const),
                    pl.BlockSpec((NE, D), const), pl.BlockSpec((BW, BW), const)],
        out_specs=[pl.BlockSpec((TM, D), row), pl.BlockSpec((TM, D), row),
                   pl.BlockSpec((NE, TM), lambda i: (0, i))],
        out_shape=[jax.ShapeDtypeStruct((M, D), F32), jax.ShapeDtypeStruct((M, D), BF16),
                   jax.ShapeDtypeStruct((NE, M), F32)],
        compiler_params=_cp(("parallel",)),
        name="merge",
    )(x, hb, yf, yb, bonus, grw, y_hy, y_ret, y_hg, w_mg, br_w, w_out, ln, mod3, g2, rt, bd)


def _topk_kernel(cap, aff_ref, ut_ref, *rest):
    rank_ref = rest[-1]
    aff = aff_ref[...]
    bits = pltpu.bitcast(aff, jnp.int32)
    thr = jnp.zeros((NE, 1), jnp.int32)
    for bit in range(30, -1, -1):
        cand = thr | (1 << bit)
        cnt = jnp.sum(jnp.where(bits >= cand, 1.0, 0.0), axis=1, keepdims=True)
        thr = jnp.where(cnt >= cap, cand, thr)
    gt = bits > thr
    eq = bits == thr
    need = cap - jnp.sum(jnp.where(gt, 1.0, 0.0), axis=1, keepdims=True)
    ut = ut_ref[...]
    ceq = _mm(jnp.where(eq, 1.0, 0.0).astype(BF16), ut)
    sel = jnp.logical_or(gt, jnp.logical_and(eq, ceq <= need))
    rank = _mm(jnp.where(sel, 1.0, 0.0).astype(BF16), ut) - 1.0
    rank_ref[...] = jnp.where(sel, rank, -1.0)


def _topk(affT, T, nb, cb0, ut, prev_out=None):
    M = affT.shape[1]
    in_specs = [pl.BlockSpec((NE, T), lambda b: (0, cb0 + b)), pl.BlockSpec((T, T), lambda b: (0, 0))]
    args = [affT, ut]
    aliases = {}
    if prev_out is not None:
        in_specs.append(pl.BlockSpec(memory_space=pl.ANY))
        args.append(prev_out)
        aliases = {2: 0}
    return pl.pallas_call(
        functools.partial(_topk_kernel, float(2 * T // NE)),
        grid=(nb,),
        in_specs=in_specs,
        out_specs=pl.BlockSpec((NE, T), lambda b: (0, cb0 + b)),
        out_shape=jax.ShapeDtypeStruct((NE, M), F32),
        input_output_aliases=aliases,
        compiler_params=_cp(("parallel",)),
        name="topk",
    )(*args)


def _gather_kernel(cap, rank_ref, h_ref, *rest):
    o_ref = rest[-1]
    T = h_ref.shape[0]
    slot = lax.broadcasted_iota(jnp.int32, (cap, T), 0).astype(F32)
    h = h_ref[...]
    for e in range(NE):
        p = jnp.where(rank_ref[e:e + 1, :] == slot, 1.0, 0.0).astype(BF16)
        o_ref[e] = _mm(p, h).astype(BF16)


def _gather(rank, h2b, T, nb, rb0, sb0, n_slots, prev_out=None):
    cap = 2 * T // NE
    in_specs = [pl.BlockSpec((NE, T), lambda b: (0, rb0 + b)), pl.BlockSpec((T, D), lambda b: (rb0 + b, 0))]
    args = [rank, h2b]
    aliases = {}
    if prev_out is not None:
        in_specs.append(pl.BlockSpec(memory_space=pl.ANY))
        args.append(prev_out)
        aliases = {2: 0}
    return pl.pallas_call(
        functools.partial(_gather_kernel, cap),
        grid=(nb,),
        in_specs=in_specs,
        out_specs=pl.BlockSpec((NE, cap, D), lambda b: (0, sb0 + b, 0)),
        out_shape=jax.ShapeDtypeStruct((NE, n_slots, D), BF16),
        input_output_aliases=aliases,
        compiler_params=_cp(("parallel",)),
        name="moe_gather",
    )(*args)


def _ffn_kernel(x_ref, w1_ref, w3_ref, w2_ref, o_ref):
    x = x_ref[0]
    acc = jnp.zeros(o_ref.shape[1:], F32)
    half = FF // 2
    for f in range(2):
        cols = slice(f * half, (f + 1) * half)
        h1 = _mm(x, w1_ref[0, :, cols])
        h3 = _mm(x, w3_ref[0, :, cols])
        acc = acc + _mm((_silu(h1) * h3).astype(BF16), w2_ref[0, cols, :])
    o_ref[0] = acc


def _ffn(xe, w1, w3, w2):
    n_slots = xe.shape[1]
    tm = 512
    while n_slots % tm:
        tm //= 2
    return pl.pallas_call(
        _ffn_kernel,
        grid=(NE, n_slots // tm),
        in_specs=[pl.BlockSpec((1, tm, D), lambda e, m: (e, m, 0)),
                  pl.BlockSpec((1, D, FF), lambda e, m: (e, 0, 0)),
                  pl.BlockSpec((1, D, FF), lambda e, m: (e, 0, 0)),
                  pl.BlockSpec((1, FF, D), lambda e, m: (e, 0, 0))],
        out_specs=pl.BlockSpec((1, tm, D), lambda e, m: (e, m, 0)),
        out_shape=jax.ShapeDtypeStruct((NE, n_slots, D), F32),
        compiler_params=_cp(("parallel", "parallel")),
        name="moe_ffn",
    )(xe, w1, w3, w2)


def _combine_kernel(cap, x_ref, ye_ref, rank_ref, gate_ref, mod_ref, o_ref):
    T = x_ref.shape[0]
    lane = lax.broadcasted_iota(jnp.int32, (T, cap), 1).astype(F32)
    acc = jnp.zeros(x_ref.shape, F32)
    for e in range(NE):
        p = jnp.where(rank_ref[:, e:e + 1] == lane, 1.0, 0.0).astype(BF16)
        acc = acc + gate_ref[:, e:e + 1] * _dot2r(p, ye_ref[e])
    o_ref[...] = x_ref[...] + mod_ref[0] * acc


def _combine(x, ye, rank_tm, gate_tm, mod3, T, nb, rb0, sb0, nP):
    cap = 2 * T // NE
    M = x.shape[0]
    tn = 256
    per = T // TM
    mod_idx = _mod_index(nP)
    return pl.pallas_call(
        functools.partial(_combine_kernel, cap),
        grid=(nb, D // tn),
        in_specs=[pl.BlockSpec((T, tn), lambda b, j: (rb0 + b, j)),
                  pl.BlockSpec((NE, cap, tn), lambda b, j: (0, sb0 + b, j)),
                  pl.BlockSpec((T, NE), lambda b, j: (rb0 + b, 0)),
                  pl.BlockSpec((T, NE), lambda b, j: (rb0 + b, 0)),
                  pl.BlockSpec((1, 1, tn),
                               lambda b, j: (mod_idx((rb0 + b) * per)[0], 0, 5 * (D // tn) + j))],
        out_specs=pl.BlockSpec((T, tn), lambda b, j: (rb0 + b, j)),
        out_shape=jax.ShapeDtypeStruct((M, D), F32),
        input_output_aliases={0: 0},
        compiler_params=_cp(("parallel", "parallel")),
        name="moe_combine",
    )(x, ye, rank_tm, gate_tm, mod3)


def _final_kernel(x_ref, g_ref, o_ref):
    x = x_ref[...]
    o_ref[...] = x * lax.rsqrt(jnp.mean(x * x, axis=-1, keepdims=True) + NORM_EPS) * g_ref[...]


def _final_norm(x, g):
    M = x.shape[0]
    return pl.pallas_call(
        _final_kernel,
        grid=(M // 512,),
        in_specs=[pl.BlockSpec((512, D), lambda i: (i, 0)), pl.BlockSpec((1, D), lambda i: (0, 0))],
        out_specs=pl.BlockSpec((512, D), lambda i: (i, 0)),
        out_shape=jax.ShapeDtypeStruct((M, D), F32),
        compiler_params=_cp(("parallel",)),
        name="final_norm",
    )(x, g)


def _rope_lane_tables(T):
    rows = T // GRID_W
    row = jnp.repeat(jnp.arange(rows, dtype=F32), GRID_W)
    col = jnp.tile(jnp.arange(GRID_W, dtype=F32), rows)
    nf = HD // 4
    inv = ROPE_BASE ** (-jnp.arange(nf, dtype=F32) / nf)
    ang = jnp.concatenate([row[:, None] * inv, col[:, None] * inv], axis=-1)
    cos, sin = jnp.cos(ang), jnp.sin(ang)
    return (jnp.tile(jnp.concatenate([cos, cos], axis=-1), (1, NH)),
            jnp.tile(jnp.concatenate([-sin, sin], axis=-1), (1, NH)))


def kernel(x_prompt, x_sample, state_rwkv, state_ret, state_hgrn, c, c_ctx, ada_w, ada_b, norm1_g, norm2_g, final_g, w_in, rw_mu, rw_w0, rw_w_up, rw_a0, rw_a_up, rw_g_up, rw_kvec, rw_ln, hy_conv, hy_ffn1, hy_ffn1_b, hy_ffn2, hy_ffn2_b, hy_ffn3, hy_freq, hy_decay, hy_skip, ret_rate, ret_gn, hg_lb, hg_norm, br_w, w_out, router, ex_w1, ex_w3, ex_w2):
    nP, nS, L = x_prompt.shape[0], x_sample.shape[0], w_in.shape[0]
    assert x_prompt.shape[1:] == (TP, D) and x_sample.shape[1:] == (TS, D)
    assert nS <= 8 and (nP * TP) % TS == 0
    MP = nP * TP
    M = MP + nS * TS
    sblk = MP // TS
    x = jnp.concatenate([x_prompt.reshape(MP, D), x_sample.reshape(nS * TS, D)], axis=0)

    cv = jnp.zeros((16, D), F32).at[:nS].set(c).at[8].set(c_ctx)
    mod = _ada(cv, ada_w, ada_b)

    lb_p = jax.nn.softmax(hg_lb.astype(F32), axis=0)
    lower_bounds = jnp.cumsum(lb_p, axis=0) - lb_p[0]

    bd = (jnp.arange(BW)[:, None] // HD == jnp.arange(BW)[None, :] // HD).astype(BF16)
    tri = _hg_tri()
    ut = {T: (jnp.arange(T)[:, None] <= jnp.arange(T)[None, :]).astype(BF16) for T in (TP, TS)}
    tabs = {T: _dft_tables(T) for T in (TP, TS)}
    feats = {T: _hy_feats(T) for T in (TP, TS)}
    cosT, sinT = _rope_lane_tables(TS)
    ones_p = jnp.ones((TP, BW), F32)
    zero_state = jnp.zeros((nP, 2, NH, HD, HD), F32)
    groups = ((TP, nP, 0, 0), (TS, nS, sblk, nP * (2 * TP // NE) // (2 * TS // NE)))
    n_slots = nP * (2 * TP // NE) + nS * (2 * TS // NE)

    rw_states, ret_states, hg_states = [], [], []
    for l in range(L):
        mod3 = mod[l].reshape(16, 1, 6 * D)
        wl = w_in[l]
        o1 = RW_COLS
        o2 = o1 + HY_COLS
        o3 = o2 + RET_COLS
        o4 = o3 + HG_COLS
        hb, z_rw, z_hy, z_ret, z_hg = _inproj(
            x, norm1_g[l][None], mod3, wl[:, :o1].astype(BF16), wl[:, o1:o2].astype(BF16),
            wl[:, o2:o3].astype(BF16), wl[:, o3:o4].astype(BF16), nP)

        wlr = jnp.zeros((128, 4 * BW), F32)
        wlr = wlr.at[0:32, 0:BW].set(rw_w_up[l, 0]).at[0:32, BW:2 * BW].set(rw_w_up[l, 1])
        wlr = wlr.at[32:64, 2 * BW:3 * BW].set(rw_a_up[l]).at[64:128, 3 * BW:].set(rw_g_up[l])
        vec = jnp.concatenate([rw_w0[l], rw_a0[l][None], rw_kvec[l], jnp.zeros((2, BW), F32)], axis=0)
        parts = _prep(z_rw, z_hy, rw_mu[l], hy_conv[l], wlr, vec, bd, nP)
        r_, wf_, wb_, kk_, kka_, k_, v_, bonus, g_rw, hx = parts

        scan_in = (r_, wf_, wb_, kk_, kka_, k_, v_)
        yf_p, yb_p, s_rw_p = _rwkv_scan_group([a[:MP] for a in scan_in], zero_state, nP, TP)
        yf_s, yb_s, _ = _rwkv_scan_group([a[MP:] for a in scan_in], state_rwkv[:, l], nS, TS)
        y_fw = jnp.concatenate([yf_p, yf_s], axis=0)
        y_bw = jnp.concatenate([yb_p, yb_s], axis=0)
        rw_states.append(s_rw_p)

        f1p = jnp.pad(hy_ffn1[l], ((0, 128 - hy_ffn1.shape[1]), (0, 0)))
        u1 = y_hy = None
        specs = {}
        for T, nb, rb0, _ in groups:
            specs[T] = _hy_spectrum(T, feats[T], tabs[T], f1p, hy_ffn1_b[l][None], hy_ffn2[l],
                                    hy_ffn2_b[l][None], hy_ffn3[l], hy_freq[l], hy_decay[l][None])
            u1 = _hy_conv(T, nb, rb0, hx, 0, hx, 1, hy_skip[l], 0, specs[T], tabs[T], u1)
        for T, nb, rb0, _ in groups:
            y_hy = _hy_conv(T, nb, rb0, u1, 0, hx, 2, hy_skip[l], 1, specs[T], tabs[T], y_hy)

        y_ret, s_ret_p = _retention(TP, nP, 0, z_ret, ones_p, ones_p, ret_rate[l].repeat(HD, axis=-1),
                                    ret_gn[l], zero_state, bd, False)
        y_ret, _ = _retention(TS, nS, sblk, z_ret, cosT, sinT, ret_rate[l].repeat(HD, axis=-1),
                              ret_gn[l], state_ret[:, l], bd, True, y_ret)
        ret_states.append(s_ret_p)
        y_hg, s_hg_p = _hgrn(TP, nP, 0, z_hg, lower_bounds[l], hg_norm[l][None], zero_state, bd, tri)
        y_hg, _ = _hgrn(TS, nS, sblk, z_hg, lower_bounds[l], hg_norm[l][None],
                        jnp.swapaxes(state_hgrn[:, l], -1, -2), bd, tri, y_hg)
        hg_states.append(jnp.swapaxes(s_hg_p, -1, -2))

        x, h2b, affT = _merge(x, hb, y_fw, y_bw, bonus, g_rw, y_hy, y_ret, y_hg,
                              wl[:, o4:].astype(BF16), br_w[l].astype(BF16), w_out[l].astype(BF16),
                              rw_ln[l], mod3, norm2_g[l][None], router[l].T, bd, nP)

        rank = xe = None
        for T, nb, rb0, sb0 in groups:
            rank = _topk(affT, T, nb, rb0, ut[T], rank)
        for T, nb, rb0, sb0 in groups:
            xe = _gather(rank, h2b, T, nb, rb0, sb0, n_slots, xe)
        ye = _ffn(xe, ex_w1[l].astype(BF16), ex_w3[l].astype(BF16), ex_w2[l].astype(BF16))
        rank_tm, gate_tm = rank.T, affT.T
        for T, nb, rb0, sb0 in groups:
            x = _combine(x, ye, rank_tm, gate_tm, mod3, T, nb, rb0, sb0, nP)

    y = _final_norm(x, final_g[None])
    pack = lambda states: jnp.stack(states, axis=1)
    return (y[:MP].reshape(nP, TP, D), y[MP:].reshape(nS, TS, D),
            pack(rw_states), pack(ret_states), pack(hg_states))
```

```python
import functools
import math

import jax
import jax.numpy as jnp
from jax import lax
from jax.experimental import pallas as pl
from jax.experimental.pallas import tpu as pltpu

F32 = jnp.float32
BF16 = jnp.bfloat16

D = 1024
BW = 256
NH = 4
HD = 64
TP = 256
TS = 2048
TM = 256
GRID_W = 64
RW_COLS = 3 * BW + 128
HY_COLS = 3 * BW
RET_COLS = 4 * BW
HG_COLS = 5 * BW
MG_COLS = 4 * D
NE = 16
FF = 2 * D
RET_CHUNK = 128
HG_CHUNK = 16
HG_GROUP = 4
HY_BANDS = 16
ROPE_BASE = 10000.0
NORM_EPS = 1e-6
RW_GN_EPS = 64e-5
GN_EPS = 1e-5
LB_FLOOR = 1e-30
SCAN_LANES = 128
SCAN_TC = 32
SCAN_VB = 32
SCAN_KU = 32
SCAN_ACC = 4
RL_TT = 16
FCH = 256
VMEM_LIMIT = 56 * 1024 * 1024


def _cp(sem, vmem=VMEM_LIMIT):
    return pltpu.CompilerParams(dimension_semantics=sem, vmem_limit_bytes=vmem)


def _sigmoid(x):
    return 1.0 / (1.0 + jnp.exp(-x))


def _silu(x):
    return x * _sigmoid(x)


def _softplus(x):
    return jnp.maximum(x, 0.0) + jnp.log(1.0 + jnp.exp(-jnp.abs(x)))


def _split(a):
    hi = a.astype(BF16)
    lo = (a - hi.astype(F32)).astype(BF16)
    return hi, lo


_NN = (((1,), (0,)), ((), ()))
_NT = (((1,), (1,)), ((), ()))
_TN = (((0,), (0,)), ((), ()))


def _mm(a, b, dims=_NN):
    return lax.dot_general(a, b, dims, preferred_element_type=F32)


def _dot3(a, b, dims=_NN):
    ah, al = _split(a)
    bh, bl = _split(b)
    return _mm(ah, bh, dims) + (_mm(ah, bl, dims) + _mm(al, bh, dims))


def _dot2l(a, b16, dims=_NN):
    ah, al = _split(a)
    return _mm(ah, b16, dims) + _mm(al, b16, dims)


def _dot2r(a16, b, dims=_NN):
    bh, bl = _split(b)
    return _mm(a16, bh, dims) + _mm(a16, bl, dims)


def _mm3(th, tl, xh, xl):
    return _mm(th, xh) + (_mm(th, xl) + _mm(tl, xh))


def _drop_ref(kern, idx):
    def wrapped(*refs):
        return kern(*refs[:idx], *refs[idx + 1:])
    return wrapped


def _ada_kernel(cv_ref, w_ref, b_ref, o_ref):
    cv = cv_ref[...]
    o_ref[0] = _dot3(_silu(cv), w_ref[0]) + b_ref[0]


def _ada(cv, ada_w, ada_b):
    L = ada_w.shape[0]
    tn = 1536
    return pl.pallas_call(
        _ada_kernel,
        grid=(L, 6 * D // tn),
        in_specs=[pl.BlockSpec((16, D), lambda l, j: (0, 0)),
                  pl.BlockSpec((1, D, tn), lambda l, j: (l, 0, j)),
                  pl.BlockSpec((1, 1, tn), lambda l, j: (l, 0, j))],
        out_specs=pl.BlockSpec((1, 16, tn), lambda l, j: (l, 0, j)),
        out_shape=jax.ShapeDtypeStruct((L, 16, 6 * D), F32),
        compiler_params=_cp(("parallel", "parallel")),
        name="ada",
    )(cv, ada_w, ada_b.reshape(L, 1, 6 * D))


def _mod_index(nP):
    def idx(i):
        return (jnp.where(i < nP, 8, (i - nP) // (TS // TM)), 0, 0)
    return idx


def _inproj_kernel(x_ref, g_ref, mod_ref, wrw_ref, why_ref, wret_ref, whg_ref,
                   hb_ref, zrw_ref, zhy_ref, zret_ref, zhg_ref):
    x = x_ref[...]
    y = x * lax.rsqrt(jnp.mean(x * x, axis=-1, keepdims=True) + NORM_EPS) * g_ref[...]
    mod = mod_ref[0]
    h = y * (1.0 + mod[:, D:2 * D]) + mod[:, 0:D]
    hb = h.astype(BF16)
    hb_ref[...] = hb
    zrw_ref[...] = _mm(hb, wrw_ref[...])
    zhy_ref[...] = _mm(hb, why_ref[...])
    zret_ref[...] = _mm(hb, wret_ref[...])
    zhg_ref[...] = _mm(hb, whg_ref[...])


def _inproj(x, g, mod3, w_rw, w_hy, w_ret, w_hg, nP):
    M = x.shape[0]
    row = lambda i: (i, 0)
    const = lambda i: (0, 0)
    widths = (RW_COLS, HY_COLS, RET_COLS, HG_COLS)
    return pl.pallas_call(
        _inproj_kernel,
        grid=(M // TM,),
        in_specs=[pl.BlockSpec((TM, D), row),
                  pl.BlockSpec((1, D), const),
                  pl.BlockSpec((1, 1, 6 * D), _mod_index(nP))]
                 + [pl.BlockSpec((D, w), const) for w in widths],
        out_specs=[pl.BlockSpec((TM, D), row)] + [pl.BlockSpec((TM, w), row) for w in widths],
        out_shape=[jax.ShapeDtypeStruct((M, D), BF16)]
                  + [jax.ShapeDtypeStruct((M, w), F32) for w in widths],
        compiler_params=_cp(("parallel",)),
        name="inproj",
    )(x, g, mod3, w_rw, w_hy, w_ret, w_hg)


def _shift_prev(z, halo8, first):
    rolled = pltpu.roll(z, 1, 0)
    halo = jnp.where(first, 0.0, halo8[7:8, :])
    row = lax.broadcasted_iota(jnp.int32, z.shape, 0)
    return jnp.where(row == 0, halo, rolled)


def _shift_next(z, halo8, last):
    n = z.shape[0]
    rolled = pltpu.roll(z, n - 1, 0)
    halo = jnp.where(last, 0.0, halo8[0:1, :])
    row = lax.broadcasted_iota(jnp.int32, z.shape, 0)
    return jnp.where(row == n - 1, halo, rolled)


def _prep_kernel(tiles, zrw_ref, zrw_p_ref, zrw_n_ref, zhy_ref, zhy_p_ref, zhy_n_ref,
                 mu_ref, hc_ref, wlr_ref, vec_ref, bd_ref, *rest):
    r_ref, wf_ref, wb_ref, kk_ref, kka_ref, k_ref, v_ref, bonus_ref, g_ref, hx_ref = rest[-10:]
    i = pl.program_id(0)
    first = i % tiles == 0
    last = i % tiles == tiles - 1

    def put(ref, val):
        ref[0] = val[:, 0:128]
        ref[1] = val[:, 128:256]

    z = zhy_ref[...]
    hc = hc_ref[...]
    hx_ref[...] = (hc[0:1] * _shift_prev(z, zhy_p_ref[...], first) + hc[1:2] * z
                   + hc[2:3] * _shift_next(z, zhy_n_ref[...], last))

    z = zrw_ref[...]
    mu = mu_ref[...]
    xr = (mu[0:1] * _shift_prev(z, zrw_p_ref[...], first) + (1.0 - mu[0:1] - mu[1:2]) * z
          + mu[1:2] * _shift_next(z, zrw_n_ref[...], last))
    r = xr[:, 0:BW]
    k = xr[:, BW:2 * BW]
    v = xr[:, 2 * BW:3 * BW]
    xl = xr[:, 3 * BW:3 * BW + 128]
    lane = lax.broadcasted_iota(jnp.int32, xl.shape, 1)
    f = jnp.where(lane < 32, jnp.tanh(xl), jnp.where(lane < 64, xl, _sigmoid(xl)))
    lr = _dot3(f, wlr_ref[...])
    vec = vec_ref[...]
    bd = bd_ref[...]
    for d, out in ((0, wf_ref), (1, wb_ref)):
        pre = vec[d:d + 1] + lr[:, d * BW:(d + 1) * BW]
        wlog = -_softplus(-pre) - 0.5
        put(out, jnp.exp(-jnp.exp(wlog)))
    a = _sigmoid(vec[2:3] + lr[:, 2 * BW:3 * BW])
    g_ref[...] = lr[:, 3 * BW:4 * BW]
    kkr = k * vec[3:4]
    nrm = jnp.sqrt(_dot2l(kkr * kkr, bd))
    kk = kkr / jnp.maximum(nrm, 1e-12)
    k2 = k * (1.0 + (a - 1.0) * vec[4:5])
    put(r_ref, r)
    put(kk_ref, kk)
    put(kka_ref, kk * a)
    put(k_ref, k2)
    put(v_ref, v)
    bonus_ref[...] = _dot2l(r * k2 * vec[5:6], bd) * v


def _prep(z_rw, z_hy, mu, hc, wlr, vec, bd, tile0, nb, T, prev=None):
    M = z_rw.shape[0]
    nb8 = M // 8
    r8 = TM // 8
    ntile = nb * T // TM
    row = lambda i: (tile0 + i, 0)
    prv = lambda i: (jnp.maximum((tile0 + i) * r8 - 1, 0), 0)
    nxt = lambda i: (jnp.minimum((tile0 + i) * r8 + r8, nb8 - 1), 0)
    const = lambda i: (0, 0)
    in_specs = [pl.BlockSpec((TM, RW_COLS), row), pl.BlockSpec((8, RW_COLS), prv),
                pl.BlockSpec((8, RW_COLS), nxt),
                pl.BlockSpec((TM, HY_COLS), row), pl.BlockSpec((8, HY_COLS), prv),
                pl.BlockSpec((8, HY_COLS), nxt),
                pl.BlockSpec((2, RW_COLS), const), pl.BlockSpec((3, HY_COLS), const),
                pl.BlockSpec((128, 4 * BW), const), pl.BlockSpec((8, BW), const),
                pl.BlockSpec((BW, BW), const)]
    args = [z_rw, z_rw, z_rw, z_hy, z_hy, z_hy, mu, hc, wlr, vec, bd]
    aliases = {}
    if prev is not None:
        for j, a in enumerate(prev):
            in_specs.append(pl.BlockSpec(memory_space=pl.ANY))
            aliases[len(args)] = 7 + j
            args.append(a)
    half = pl.BlockSpec((2, TM, 128), lambda i: (0, i, 0))
    return pl.pallas_call(
        functools.partial(_prep_kernel, T // TM),
        grid=(ntile,),
        in_specs=in_specs,
        out_specs=[half] * 7 + [pl.BlockSpec((TM, BW), row)] * 2 + [pl.BlockSpec((TM, HY_COLS), row)],
        out_shape=[jax.ShapeDtypeStruct((2, ntile * TM, 128), F32)] * 7
                  + [jax.ShapeDtypeStruct((M, BW), F32)] * 2 + [jax.ShapeDtypeStruct((M, HY_COLS), F32)],
        input_output_aliases=aliases,
        compiler_params=_cp(("parallel",)),
        name="prep",
    )(*args)


def _rl_in_kernel(G, tt, r_ref, kk_ref, kka_ref, k_ref, wf_ref, wb_ref, v_ref,
                  ro_ref, kko_ref, kkao_ref, ko_ref, wo_ref, vo_ref):
    vp = HD // G

    def head_pieces(ref, tl, lo, width):
        halves = [ref[hh, :, tl, :] for hh in range(2)]
        return [halves[h // 2][:, (h % 2) * HD + lo:(h % 2) * HD + lo + width] for h in range(NH)]

    def body(tl, carry):
        def rows(ref):
            return jnp.concatenate(head_pieces(ref, tl, 0, HD) * G, axis=0).T
        ro_ref[tl] = rows(r_ref)
        kko_ref[tl] = rows(kk_ref)
        kkao_ref[tl] = rows(kka_ref)
        ko_ref[tl] = rows(k_ref)
        wo_ref[0, tl] = rows(wf_ref)
        wo_ref[1, tl] = rows(wb_ref)
        pieces = []
        for g in range(G):
            pieces += head_pieces(v_ref, tl, g * vp, vp)
        vo_ref[tl] = jnp.concatenate(pieces, axis=0).T
        return carry

    lax.fori_loop(0, tt, body, 0)


def _rl_in(parts, nb, T, G):
    vp = HD // G
    tt = RL_TT
    src = pl.BlockSpec((2, nb, tt, 128), lambda i: (0, 0, i, 0))
    rows = pl.BlockSpec((tt, HD, SCAN_LANES), lambda i: (i, 0, 0))
    return pl.pallas_call(
        functools.partial(_rl_in_kernel, G, tt),
        grid=(T // tt,),
        in_specs=[src] * 7,
        out_specs=[rows] * 4 + [pl.BlockSpec((2, tt, HD, SCAN_LANES), lambda i: (0, i, 0, 0)),
                                pl.BlockSpec((tt, vp, SCAN_LANES), lambda i: (i, 0, 0))],
        out_shape=[jax.ShapeDtypeStruct((T, HD, SCAN_LANES), F32)] * 4
                  + [jax.ShapeDtypeStruct((2, T, HD, SCAN_LANES), F32),
                     jax.ShapeDtypeStruct((T, vp, SCAN_LANES), F32)],
        compiler_params=_cp(("parallel",)),
        name="rwkv_layout_in",
    )(*[a.reshape(2, nb, T, 128) for a in parts])


def _rl_out_kernel(G, nb, tt, y_ref, yf_ref, yb_ref):
    def body(tl, carry):
        for d, out in ((0, yf_ref), (1, yb_ref)):
            z = y_ref[d, tl].T
            for hh in range(2):
                out[hh, :, tl, :] = jnp.concatenate(
                    [z[(g * NH + h) * nb:(g * NH + h + 1) * nb, :] for h in (2 * hh, 2 * hh + 1) for g in range(G)],
                    axis=1)
        return carry

    lax.fori_loop(0, tt, body, 0)


def _rl_out(y, nb, T, G):
    vp = HD // G
    tt = RL_TT
    dst = pl.BlockSpec((2, nb, tt, 128), lambda i: (0, 0, i, 0))
    yf, yb = pl.pallas_call(
        functools.partial(_rl_out_kernel, G, nb, tt),
        grid=(T // tt,),
        in_specs=[pl.BlockSpec((2, tt, vp, SCAN_LANES), lambda i: (0, i, 0, 0))],
        out_specs=[dst, dst],
        out_shape=[jax.ShapeDtypeStruct((2, nb, T, 128), F32)] * 2,
        compiler_params=_cp(("parallel",)),
        name="rwkv_layout_out",
    )(y)
    return yf.reshape(2, nb * T, 128), yb.reshape(2, nb * T, 128)


def _scan_kernel(nc, r_ref, w_ref, kk_ref, kka_ref, k_ref, v_ref, s0_ref, y_ref, sT_ref, s_scr):
    d = pl.program_id(0)
    c = pl.program_id(1)

    @pl.when(c == 0)
    def _():
        s_scr[...] = s0_ref[0]

    vp = v_ref.shape[1]
    vb = min(vp, SCAN_VB)

    def step(i, carry):
        t = i + d * (SCAN_TC - 1 - 2 * i)
        for v0 in range(0, vp, vb):
            vs = slice(v0, v0 + vb)
            vt = v_ref[t, vs, :]

            zero = jnp.zeros((vb, SCAN_LANES), F32)
            acc0 = (zero,) * SCAN_ACC

            def dot_kk(j, acc):
                out = []
                for a in range(SCAN_ACC):
                    kc = j * SCAN_ACC + a
                    out.append(acc[a] + s_scr[kc, vs, :] * kk_ref[t, pl.ds(kc, 1), :])
                return tuple(out)

            sa = sum(lax.fori_loop(0, HD // SCAN_ACC, dot_kk, acc0, unroll=SCAN_KU // SCAN_ACC))

            def update(j, acc):
                out = []
                for a in range(SCAN_ACC):
                    kc = j * SCAN_ACC + a
                    row = pl.ds(kc, 1)
                    s_new = (s_scr[kc, vs, :] * w_ref[0, t, row, :]
                             + (vt * k_ref[t, row, :] - sa * kka_ref[t, row, :]))
                    s_scr[kc, vs, :] = s_new
                    out.append(acc[a] + s_new * r_ref[t, row, :])
                return tuple(out)

            y_ref[0, t, vs, :] = sum(lax.fori_loop(0, HD // SCAN_ACC, update, acc0,
                                                   unroll=SCAN_KU // SCAN_ACC))
        return carry

    lax.fori_loop(0, SCAN_TC, step, 0)

    @pl.when(c == nc - 1)
    def _():
        sT_ref[0] = s_scr[...]


def _scan(r, w, kk, kka, k, v, s0):
    T = r.shape[0]
    vp = v.shape[1]
    nc = T // SCAN_TC
    tb = lambda d, c: c + d * (nc - 1 - 2 * c)
    rows = pl.BlockSpec((SCAN_TC, HD, SCAN_LANES), lambda d, c: (tb(d, c), 0, 0))
    wrows = pl.BlockSpec((1, SCAN_TC, HD, SCAN_LANES), lambda d, c: (d, tb(d, c), 0, 0))
    vrows = pl.BlockSpec((SCAN_TC, vp, SCAN_LANES), lambda d, c: (tb(d, c), 0, 0))
    yrows = pl.BlockSpec((1, SCAN_TC, vp, SCAN_LANES), lambda d, c: (d, tb(d, c), 0, 0))
    st = pl.BlockSpec((1, HD, vp, SCAN_LANES), lambda d, c: (d, 0, 0, 0))
    return pl.pallas_call(
        functools.partial(_scan_kernel, nc),
        grid=(2, nc),
        in_specs=[rows, wrows, rows, rows, rows, vrows, st],
        out_specs=[yrows, st],
        out_shape=[jax.ShapeDtypeStruct((2, T, vp, SCAN_LANES), F32),
                   jax.ShapeDtypeStruct((2, HD, vp, SCAN_LANES), F32)],
        scratch_shapes=[pltpu.VMEM((HD, vp, SCAN_LANES), F32)],
        compiler_params=_cp(("parallel", "arbitrary")),
        name="rwkv_scan",
    )(r, w, kk, kka, k, v, s0)


def _to_scan_state(s, nb, G):
    vp = HD // G
    a = s.reshape(nb, 2, NH, G, vp, HD)
    return a.transpose(1, 5, 4, 3, 2, 0).reshape(2, HD, vp, G * NH * nb)


def _from_scan_state(s, nb, G):
    vp = HD // G
    a = s.reshape(2, HD, vp, G, NH, nb).transpose(5, 0, 4, 3, 2, 1)
    return a.reshape(nb, 2, NH, HD, HD)


def _rwkv_scan_group(parts, s0, nb, T):
    r, wf, wb, kk, kka, k, v = parts
    assert SCAN_LANES % (nb * NH) == 0
    G = SCAN_LANES // (nb * NH)
    r_s, kk_s, kka_s, k_s, w_s, v_s = _rl_in((r, kk, kka, k, wf, wb, v), nb, T, G)
    y, sT = _scan(r_s, w_s, kk_s, kka_s, k_s, v_s, _to_scan_state(s0, nb, G))
    y_fw, y_bw = _rl_out(y, nb, T, G)
    return y_fw, y_bw, _from_scan_state(sT, nb, G)


def _dft_tables(T):
    N = 2 * T
    i = jnp.arange(T, dtype=jnp.int32)
    prod = ((2 * i[:, None] + 1) * (2 * i[None, :] + 1)) % (4 * N)
    ang = prod.astype(F32) * (2.0 * math.pi / (4 * N))
    half = (2 * i + 1).astype(F32) * (math.pi / (2 * N))
    ph = jnp.zeros((T, 128), F32).at[:, 0].set(jnp.cos(half)).at[:, 1].set(jnp.sin(half))
    return _split(jnp.cos(ang)) + _split(jnp.sin(ang)) + (ph,)


def _hy_feats(T):
    t = jnp.linspace(0.0, 1.0, T, dtype=F32)[:, None]
    w = (2.0 * math.pi / T) * jnp.arange(T, dtype=F32)[:, None]
    bands = jnp.linspace(1e-4, HY_BANDS - 1.0, HY_BANDS, dtype=F32)[None, :]
    feats = jnp.concatenate([t, jnp.cos(bands * w), -jnp.sin(bands * w)], axis=-1)
    return jnp.pad(feats, ((0, 0), (0, 128 - feats.shape[1])))


def _spec_kernel(T, feats_ref, f1_ref, b1_ref, f2_ref, b2_ref, f3_ref, freq_ref, dec_ref,
                 ch_ref, cl_ref, sh_ref, sl_ref, ph_ref, re_ref, im_ref, fh_scr, fl_scr):
    j = pl.program_id(0)

    @pl.when(j == 0)
    def _():
        feats = feats_ref[...]
        freq = freq_ref[...]
        h1 = jnp.sin(freq[0:1] * (_dot3(feats, f1_ref[...]) + b1_ref[...]))
        h2 = jnp.sin(freq[1:2] * (_dot3(h1, f2_ref[...]) + b2_ref[...]))
        filt = _dot3(h2, f3_ref[...]) * jnp.exp(-feats[:, 0:1] * jnp.abs(dec_ref[...]))
        row = lax.broadcasted_iota(jnp.int32, (T, BW), 0)
        for o in range(2):
            fw = filt[:, o * 2 * BW:o * 2 * BW + BW]
            bw = filt[:, o * 2 * BW + BW:(o + 1) * 2 * BW]
            den = jnp.sum(jnp.abs(fw) + jnp.abs(bw), axis=0, keepdims=True)
            fw = fw / den
            bw0 = jnp.where(row == 0, 0.0, bw / den)
            for col, val in ((o * BW, fw + bw0), (2 * BW + o * BW, fw - bw0)):
                hi, lo = _split(val)
                fh_scr[:, col:col + BW] = hi
                fl_scr[:, col:col + BW] = lo

    fh = fh_scr[...]
    fl = fl_scr[...]
    c = _mm3(ch_ref[...], cl_ref[...], fh, fl)
    s = _mm3(sh_ref[...], sl_ref[...], fh, fl)
    pc = ph_ref[:, 0:1] * (1.0 / T)
    ps = ph_ref[:, 1:2] * (1.0 / T)
    re_ref[...] = pc * c[:, 0:2 * BW] + ps * s[:, 0:2 * BW]
    im_ref[...] = ps * c[:, 2 * BW:4 * BW] - pc * s[:, 2 * BW:4 * BW]


def _hy_spectrum(T, feats, tabs, f1p, b1, f2, b2, f3, freq, dec):
    ch, cl, sh, sl, ph = tabs
    fch = min(FCH, T)
    const = lambda j: (0, 0)
    rowc = pl.BlockSpec((fch, T), lambda j: (j, 0))
    return pl.pallas_call(
        functools.partial(_spec_kernel, T),
        grid=(T // fch,),
        in_specs=[pl.BlockSpec((T, 128), const), pl.BlockSpec((128, 64), const),
                  pl.BlockSpec((1, 64), const), pl.BlockSpec((64, 64), const),
                  pl.BlockSpec((1, 64), const), pl.BlockSpec((64, 4 * BW), const),
                  pl.BlockSpec((2, 64), const), pl.BlockSpec((1, 4 * BW), const),
                  rowc, rowc, rowc, rowc, pl.BlockSpec((fch, 128), lambda j: (j, 0))],
        out_specs=[pl.BlockSpec((fch, 2 * BW), lambda j: (j, 0))] * 2,
        out_shape=[jax.ShapeDtypeStruct((T, 2 * BW), F32)] * 2,
        scratch_shapes=[pltpu.VMEM((T, 4 * BW), BF16), pltpu.VMEM((T, 4 * BW), BF16)],
        compiler_params=_cp(("arbitrary",)),
        name="hy_spectrum",
    )(feats, f1p, b1, f2, b2, f3, freq, dec, ch, cl, sh, sl, ph)


def _conv_kernel(nj, u_ref, x_ref, skip_ref, sre_ref, sim_ref,
                 cr_ref, sr_ref, cc_ref, sc_ref, *rest):
    o_ref, u_scr, acc_scr = rest[-3:]
    j = pl.program_id(1)

    @pl.when(j == 0)
    def _():
        u_scr[...] = u_ref[...].astype(BF16)
        acc_scr[...] = jnp.zeros_like(acc_scr)

    ub = u_scr[...]
    uc = _mm(cr_ref[...], ub)
    us = _mm(sr_ref[...], ub)
    sre = sre_ref[...]
    sim = sim_ref[...]
    yre = (uc * sre + us * sim).astype(BF16)
    yim = (uc * sim - us * sre).astype(BF16)
    acc_scr[...] += _mm(cc_ref[...], yre) - _mm(sc_ref[...], yim)

    @pl.when(j == nj - 1)
    def _():
        u = u_ref[...]
        o_ref[...] = x_ref[...] * (acc_scr[...] + skip_ref[...] * u)


def _hy_conv(T, nb, rb0, u_arr, cu, hx, cx, skip, order, spec, tabs, prev_out=None):
    ch, cl, sh, sl, _ = tabs
    sre, sim = spec
    fch = min(FCH, T)
    nj = T // fch
    M = hx.shape[0]
    rowc = pl.BlockSpec((fch, T), lambda b, j: (j, 0))
    colc = pl.BlockSpec((T, fch), lambda b, j: (0, j))
    sp = pl.BlockSpec((fch, BW), lambda b, j: (j, order))
    in_specs = [pl.BlockSpec((T, BW), lambda b, j: (rb0 + b, cu)),
                pl.BlockSpec((T, BW), lambda b, j: (rb0 + b, cx)),
                pl.BlockSpec((1, BW), lambda b, j: (0, 0)),
                sp, sp, rowc, rowc, colc, colc]
    args = [u_arr, hx, skip[order][None], sre, sim, ch, sh, ch, sh]
    aliases = {}
    if prev_out is not None:
        in_specs.append(pl.BlockSpec(memory_space=pl.ANY))
        args.append(prev_out)
        aliases = {len(args) - 1: 0}
    return pl.pallas_call(
        functools.partial(_conv_kernel, nj),
        grid=(nb, nj),
        in_specs=in_specs,
        out_specs=pl.BlockSpec((T, BW), lambda b, j: (rb0 + b, 0)),
        out_shape=jax.ShapeDtypeStruct((M, BW), F32),
        scratch_shapes=[pltpu.VMEM((T, BW), BF16), pltpu.VMEM((T, BW), F32)],
        input_output_aliases=aliases,
        compiler_params=_cp(("parallel", "arbitrary")),
        name="hy_conv",
    )(*args)


def _head_ln(y, bd, gain, bias, eps):
    mu = _dot2l(y, bd) * (1.0 / HD)
    yc = y - mu
    var = _dot2l(yc * yc, bd) * (1.0 / HD)
    return yc * lax.rsqrt(var + eps) * gain + bias


def _ret_kernel(T, use_rope, z_ref, cos_ref, sin_ref, rate_ref, gn_ref, s0_ref, bd_ref,
                y_ref, sT_ref, q_scr, k_scr, y_scr, s_scr, dm_scr):
    C = RET_CHUNK
    nc = T // C
    q = z_ref[:, 0:BW]
    k = z_ref[:, BW:2 * BW]
    if use_rope:
        lane = lax.broadcasted_iota(jnp.int32, (T, BW), 1) % HD
        cos = cos_ref[...]
        sin = sin_ref[...]

        def rope(x):
            sw = jnp.where(lane < HD // 2, pltpu.roll(x, BW - HD // 2, 1), pltpu.roll(x, HD // 2, 1))
            return x * cos + sw * sin
        q = rope(q)
        k = rope(k)
    q_scr[...] = q * (HD ** -0.5)
    k_scr[...] = k
    s_scr[...] = s0_ref[0]

    lg = -jnp.exp(rate_ref[...])
    n_col = lax.broadcasted_iota(jnp.int32, (C, BW), 0).astype(F32)
    n_r = lax.broadcasted_iota(jnp.int32, (C, C), 0)
    n_c = lax.broadcasted_iota(jnp.int32, (C, C), 1)
    qdec = (jnp.exp((n_col + 1.0) * lg[0:1]), jnp.exp((C - n_col) * lg[1:2]))
    kdec = (jnp.exp((C - 1.0 - n_col) * lg[0:1]), jnp.exp(n_col * lg[1:2]))
    cdec = jnp.exp(float(C) * lg)
    for d in range(2):
        rel = (n_r - n_c) if d == 0 else (n_c - n_r)
        relf = jnp.maximum(rel, 0).astype(F32)
        for h in range(NH):
            lgh = lg[d:d + 1, h * HD:h * HD + 1]
            dm_scr[d * NH + h] = jnp.where(rel >= 0, jnp.exp(relf * lgh), 0.0)

    def both(i, carry):
        units = []
        for d in range(2):
            rows = pl.ds(pl.multiple_of((i if d == 0 else nc - 1 - i) * C, C), C)
            qc = q_scr[rows, :]
            kc = k_scr[rows, :]
            qb = qc.astype(BF16)
            kb = kc.astype(BF16)
            qd = (qc * qdec[d]).astype(BF16)
            kd = (kc * kdec[d]).astype(BF16)
            vb = z_ref[rows, 2 * BW:3 * BW].astype(BF16)
            for h in range(NH):
                hs = slice(h * HD, (h + 1) * HD)
                s = s_scr[d, h]
                units.append((d, h, rows, vb[:, hs], s,
                              _mm(qb[:, hs], kb[:, hs], _NT),
                              _mm(qd[:, hs], s.astype(BF16)),
                              _mm(kd[:, hs], vb[:, hs], _TN)))
        ys = [[], []]
        for d, h, rows, vh, s, att, inter, upd in units:
            att = (att * dm_scr[d * NH + h]).astype(BF16)
            ys[d].append(_mm(att, vh) + inter)
            s_scr[d, h] = s * cdec[d:d + 1, h * HD:h * HD + 1] + upd
        for d in range(2):
            y_scr[d, units[d * NH][2], :] = jnp.concatenate(ys[d], axis=1)
        return carry

    lax.fori_loop(0, nc, both, 0)

    gn = gn_ref[...]
    y = _head_ln(y_scr[0] + y_scr[1], bd_ref[...], gn[0:1], gn[1:2], GN_EPS)
    y_ref[...] = y * _silu(z_ref[:, 3 * BW:4 * BW])
    sT_ref[0] = s_scr[...]


def _retention(T, nb, rb0, z_ret, cosT, sinT, rate_l, gn, s0, bd, use_rope, prev_out=None):
    M = z_ret.shape[0]
    const = lambda b: (0, 0)
    in_specs = [pl.BlockSpec((T, RET_COLS), lambda b: (rb0 + b, 0)),
                pl.BlockSpec((T, BW), const), pl.BlockSpec((T, BW), const),
                pl.BlockSpec((2, BW), const), pl.BlockSpec((2, BW), const),
                pl.BlockSpec((1, 2, NH, HD, HD), lambda b: (b, 0, 0, 0, 0)),
                pl.BlockSpec((BW, BW), const)]
    args = [z_ret, cosT, sinT, rate_l, gn, s0, bd]
    aliases = {}
    if prev_out is not None:
        in_specs.append(pl.BlockSpec(memory_space=pl.ANY))
        args.append(prev_out)
        aliases = {len(args) - 1: 0}
    kern = functools.partial(_ret_kernel, T, use_rope)
    if prev_out is not None:
        kern = _drop_ref(kern, len(args) - 1)
    return pl.pallas_call(
        kern,
        grid=(nb,),
        in_specs=in_specs,
        out_specs=[pl.BlockSpec((T, BW), lambda b: (rb0 + b, 0)),
                   pl.BlockSpec((1, 2, NH, HD, HD), lambda b: (b, 0, 0, 0, 0))],
        out_shape=[jax.ShapeDtypeStruct((M, BW), F32),
                   jax.ShapeDtypeStruct((nb, 2, NH, HD, HD), F32)],
        scratch_shapes=[pltpu.VMEM((T, BW), F32), pltpu.VMEM((T, BW), F32), pltpu.VMEM((2, T, BW), F32),
                        pltpu.VMEM((2, NH, HD, HD), F32), pltpu.VMEM((2 * NH, RET_CHUNK, RET_CHUNK), F32)],
        input_output_aliases=aliases,
        compiler_params=_cp(("parallel",)),
        name="retention",
    )(*args)


def _hg_kernel(T, z_ref, lb_ref, norm_ref, s0_ref, bd_ref, tri_ref,
               y_ref, sT_ref, q_scr, kin_scr, lf_scr, y_scr, st_scr):
    C = HG_CHUNK
    nc = T // C
    q_scr[...] = _silu(z_ref[:, 0:BW])
    for d in range(2):
        zf = z_ref[:, (1 + d) * BW:(2 + d) * BW]
        lb = lb_ref[d:d + 1, :]
        la = jnp.log(jnp.maximum(lb, LB_FLOOR)) + jnp.zeros_like(zf)
        lc = jnp.log(1.0 - lb) - _softplus(-zf)
        lf_scr[d] = jnp.maximum(la, lc) + jnp.log(1.0 + jnp.exp(-jnp.abs(la - lc)))
        kin_scr[d] = (1.0 - lb) * _sigmoid(-zf)
    st_scr[...] = s0_ref[0]
    bd = bd_ref[...]
    row = lax.broadcasted_iota(jnp.int32, (C, BW), 0)
    G = HG_GROUP
    R = G * C
    ng = T // R

    def shifted(a, o, d):
        if o == 0:
            return a
        return pltpu.roll(a, o if d == 0 else C - o, 0)

    def group(i, carry):
        dirs = []
        for d in range(2):
            rows = pl.ds(pl.multiple_of((i if d == 0 else ng - 1 - i) * R, R), R)
            g = lf_scr[d, rows, :]
            g1 = g.astype(BF16)
            g2 = (g - g1.astype(F32)).astype(BF16)
            g3 = (g - g1.astype(F32) - g2.astype(F32)).astype(BF16)
            tri = tri_ref[d]
            b = _mm(tri, g1) + (_mm(tri, g2) + _mm(tri, g3))
            dirs.append((rows, b, q_scr[rows, :], kin_scr[d, rows, :], z_ref[rows, 3 * BW:4 * BW]))

        atts = []
        for d, (rows, b, q, kin, v) in enumerate(dirs):
            ps = []
            for c in range(G):
                cs = slice(c * C, (c + 1) * C)
                bc, qc, kc = b[cs], q[cs], kin[cs]
                for o in range(C):
                    mask = (row >= o) if d == 0 else (row <= C - 1 - o)
                    e = jnp.exp(jnp.minimum(bc - shifted(bc, o, d), 0.0))
                    ps.append(jnp.where(mask, e * shifted(kc, o, d) * qc, 0.0).astype(BF16))
            atts.append(_mm(jnp.concatenate(ps, axis=0), bd))

        upd = []
        for d, (rows, b, q, kin, v) in enumerate(dirs):
            per_chunk = []
            for c in range(G):
                cs = slice(c * C, (c + 1) * C)
                bc = b[cs]
                b_end = bc[C - 1:C, :] if d == 0 else bc[0:1, :]
                kd = (kin[cs] * jnp.exp(b_end - bc)).astype(BF16)
                vb = v[cs].astype(BF16)
                u = [_mm(vb[:, h * HD:(h + 1) * HD], kd[:, h * HD:(h + 1) * HD], _TN) for h in range(NH)]
                per_chunk.append((u, jnp.exp(b_end), (q[cs] * jnp.exp(bc)).astype(BF16)))
            upd.append(per_chunk)

        for d, (rows, b, q, kin, v) in enumerate(dirs):
            att = atts[d]
            ys = [None] * G
            for c in range(G):
                vc = v[c * C:(c + 1) * C]
                y = jnp.zeros((C, BW), F32)
                for o in range(C):
                    y = y + att[(c * C + o) * C:(c * C + o + 1) * C, :] * shifted(vc, o, d)
                ys[c] = y
            st = [st_scr[d, h] for h in range(NH)]
            for c in (range(G) if d == 0 else range(G - 1, -1, -1)):
                u, eb, qe = upd[d][c]
                inter = []
                for h in range(NH):
                    hs = slice(h * HD, (h + 1) * HD)
                    inter.append(_mm(qe[:, hs], st[h].astype(BF16), _NT))
                    st[h] = eb[:, hs] * st[h] + u[h]
                ys[c] = ys[c] + jnp.concatenate(inter, axis=1)
            for h in range(NH):
                st_scr[d, h] = st[h]
            y_scr[d, rows, :] = jnp.concatenate(ys, axis=0)
        return carry

    lax.fori_loop(0, ng, group, 0)

    y = y_scr[0] + y_scr[1]
    ms = _dot2l(y * y, bd) * (1.0 / HD)
    y_ref[...] = y * lax.rsqrt(ms + NORM_EPS) * norm_ref[...] * _silu(z_ref[:, 4 * BW:5 * BW])
    sT_ref[0] = st_scr[...]


def _hg_tri():
    ci = jnp.arange(HG_CHUNK * HG_GROUP)
    same = ci[:, None] // HG_CHUNK == ci[None, :] // HG_CHUNK
    return jnp.stack([same & (ci[:, None] >= ci[None, :]), same & (ci[:, None] <= ci[None, :])]).astype(BF16)


def _hgrn(T, nb, rb0, z_hg, lb, norm, s0t, bd, tri, prev_out=None):
    M = z_hg.shape[0]
    const = lambda b: (0, 0)
    in_specs = [pl.BlockSpec((T, HG_COLS), lambda b: (rb0 + b, 0)),
                pl.BlockSpec((2, BW), const), pl.BlockSpec((1, BW), const),
                pl.BlockSpec((1, 2, NH, HD, HD), lambda b: (b, 0, 0, 0, 0)),
                pl.BlockSpec((BW, BW), const),
                pl.BlockSpec((2, HG_CHUNK * HG_GROUP, HG_CHUNK * HG_GROUP), lambda b: (0, 0, 0))]
    args = [z_hg, lb, norm, s0t, bd, tri]
    aliases = {}
    if prev_out is not None:
        in_specs.append(pl.BlockSpec(memory_space=pl.ANY))
        args.append(prev_out)
        aliases = {len(args) - 1: 0}
    kern = functools.partial(_hg_kernel, T)
    if prev_out is not None:
        kern = _drop_ref(kern, len(args) - 1)
    return pl.pallas_call(
        kern,
        grid=(nb,),
        in_specs=in_specs,
        out_specs=[pl.BlockSpec((T, BW), lambda b: (rb0 + b, 0)),
                   pl.BlockSpec((1, 2, NH, HD, HD), lambda b: (b, 0, 0, 0, 0))],
        out_shape=[jax.ShapeDtypeStruct((M, BW), F32),
                   jax.ShapeDtypeStruct((nb, 2, NH, HD, HD), F32)],
        scratch_shapes=[pltpu.VMEM((T, BW), F32), pltpu.VMEM((2, T, BW), F32),
                        pltpu.VMEM((2, T, BW), F32), pltpu.VMEM((2, T, BW), F32),
                        pltpu.VMEM((2, NH, HD, HD), F32)],
        input_output_aliases=aliases,
        compiler_params=_cp(("parallel",)),
        name="hgrn2",
    )(*args)


def _merge_kernel(nP, x_ref, hb_ref, yfp_ref, ybp_ref, yfs_ref, ybs_ref, bonus_ref, grw_ref,
                  yhy_ref, yret_ref, yhg_ref,
                  wmg_ref, brw_ref, wout_ref, ln_ref, mod_ref, g2_ref, rt_ref, bd_ref,
                  xo_ref, h2_ref, aff_ref):
    bd = bd_ref[...]
    ln = ln_ref[...]
    is_prompt = pl.program_id(0) < nP
    halves = [jnp.where(is_prompt, yfp_ref[hh] + ybp_ref[hh], yfs_ref[hh] + ybs_ref[hh]) for hh in range(2)]
    y_sum = jnp.concatenate(halves, axis=1)
    y_rw = (_head_ln(y_sum, bd, ln[0:1], ln[1:2], RW_GN_EPS)
            + bonus_ref[...]) * grw_ref[...]
    hb = hb_ref[...]
    merged = jnp.zeros((TM, D), F32)
    for n, y in enumerate((y_rw, yhy_ref[...], yret_ref[...], yhg_ref[...])):
        gate = _sigmoid(_mm(hb, wmg_ref[:, n * D:(n + 1) * D]))
        merged = merged + gate * _mm(y.astype(BF16), brw_ref[n])
    out = _mm(merged.astype(BF16), wout_ref[...])
    mod = mod_ref[0]
    x = x_ref[...] + mod[:, 2 * D:3 * D] * out
    xo_ref[...] = x
    y = x * lax.rsqrt(jnp.mean(x * x, axis=-1, keepdims=True) + NORM_EPS) * g2_ref[...]
    h2 = y * (1.0 + mod[:, 4 * D:5 * D]) + mod[:, 3 * D:4 * D]
    h2_ref[...] = h2.astype(BF16)
    logits = _dot3(rt_ref[...], h2, _NT)
    m = jnp.max(logits, axis=0, keepdims=True)
    e = jnp.exp(logits - m)
    aff_ref[...] = e / jnp.sum(e, axis=0, keepdims=True)


def _merge(x, hb, y_p, y_s, bonus, grw, y_hy, y_ret, y_hg, w_mg, br_w, w_out, ln, mod3, g2, rt, bd, nP):
    M = x.shape[0]
    row = lambda i: (i, 0)
    const = lambda i: (0, 0)
    small = pl.BlockSpec((TM, BW), row)
    ntp = y_p[0].shape[1] // TM
    half_p = pl.BlockSpec((2, TM, 128), lambda i: (0, jnp.minimum(i, ntp - 1), 0))
    half_s = pl.BlockSpec((2, TM, 128), lambda i: (0, jnp.maximum(i - ntp, 0), 0))
    assert ntp == nP
    return pl.pallas_call(
        functools.partial(_merge_kernel, nP),
        grid=(M // TM,),
        in_specs=[pl.BlockSpec((TM, D), row), pl.BlockSpec((TM, D), row), half_p, half_p, half_s, half_s]
                 + [small] * 5
                 + [pl.BlockSpec((D, MG_COLS), const), pl.BlockSpec((4, BW, D), lambda i: (0, 0, 0)),
                    pl.BlockSpec((D, D), const), pl.BlockSpec((2, BW), const),
                    pl.BlockSpec((1, 1, 6 * D), _mod_index(nP)), pl.BlockSpec((1, D), const),
                    pl.BlockSpec((NE, D), const), pl.BlockSpec((BW, BW), const)],
        out_specs=[pl.BlockSpec((TM, D), row), pl.BlockSpec((TM, D), row),
                   pl.BlockSpec((NE, TM), lambda i: (0, i))],
        out_shape=[jax.ShapeDtypeStruct((M, D), F32), jax.ShapeDtypeStruct((M, D), BF16),
                   jax.ShapeDtypeStruct((NE, M), F32)],
        compiler_params=_cp(("parallel",)),
        name="merge",
    )(x, hb, y_p[0], y_p[1], y_s[0], y_s[1], bonus, grw, y_hy, y_ret, y_hg,
      w_mg, br_w, w_out, ln, mod3, g2, rt, bd)


def _topk_kernel(cap, aff_ref, ut_ref, *rest):
    rank_ref = rest[-1]
    aff = aff_ref[...]
    bits = pltpu.bitcast(aff, jnp.int32)
    thr = jnp.zeros((NE, 1), jnp.int32)
    for bit in range(30, -1, -1):
        cand = thr | (1 << bit)
        cnt = jnp.sum(jnp.where(bits >= cand, 1.0, 0.0), axis=1, keepdims=True)
        thr = jnp.where(cnt >= cap, cand, thr)
    gt = bits > thr
    eq = bits == thr
    need = cap - jnp.sum(jnp.where(gt, 1.0, 0.0), axis=1, keepdims=True)
    ut = ut_ref[...]
    ceq = _mm(jnp.where(eq, 1.0, 0.0).astype(BF16), ut)
    sel = jnp.logical_or(gt, jnp.logical_and(eq, ceq <= need))
    rank = _mm(jnp.where(sel, 1.0, 0.0).astype(BF16), ut) - 1.0
    rank_ref[...] = jnp.where(sel, rank, -1.0)


def _topk(affT, T, nb, cb0, ut, prev_out=None):
    M = affT.shape[1]
    in_specs = [pl.BlockSpec((NE, T), lambda b: (0, cb0 + b)), pl.BlockSpec((T, T), lambda b: (0, 0))]
    args = [affT, ut]
    aliases = {}
    if prev_out is not None:
        in_specs.append(pl.BlockSpec(memory_space=pl.ANY))
        args.append(prev_out)
        aliases = {2: 0}
    return pl.pallas_call(
        functools.partial(_topk_kernel, float(2 * T // NE)),
        grid=(nb,),
        in_specs=in_specs,
        out_specs=pl.BlockSpec((NE, T), lambda b: (0, cb0 + b)),
        out_shape=jax.ShapeDtypeStruct((NE, M), F32),
        input_output_aliases=aliases,
        compiler_params=_cp(("parallel",)),
        name="topk",
    )(*args)


def _gather_kernel(cap, rank_ref, h_ref, *rest):
    o_ref = rest[-1]
    T = h_ref.shape[0]
    slot = lax.broadcasted_iota(jnp.int32, (cap, T), 0).astype(F32)
    h = h_ref[...]
    for e in range(NE):
        p = jnp.where(rank_ref[e:e + 1, :] == slot, 1.0, 0.0).astype(BF16)
        o_ref[e] = _mm(p, h).astype(BF16)


def _gather(rank, h2b, T, nb, rb0, sb0, n_slots, prev_out=None):
    cap = 2 * T // NE
    in_specs = [pl.BlockSpec((NE, T), lambda b: (0, rb0 + b)), pl.BlockSpec((T, D), lambda b: (rb0 + b, 0))]
    args = [rank, h2b]
    aliases = {}
    if prev_out is not None:
        in_specs.append(pl.BlockSpec(memory_space=pl.ANY))
        args.append(prev_out)
        aliases = {2: 0}
    return pl.pallas_call(
        functools.partial(_gather_kernel, cap),
        grid=(nb,),
        in_specs=in_specs,
        out_specs=pl.BlockSpec((NE, cap, D), lambda b: (0, sb0 + b, 0)),
        out_shape=jax.ShapeDtypeStruct((NE, n_slots, D), BF16),
        input_output_aliases=aliases,
        compiler_params=_cp(("parallel",)),
        name="moe_gather",
    )(*args)


def _ffn_kernel(x_ref, w1_ref, w3_ref, w2_ref, o_ref):
    x = x_ref[0]
    acc = jnp.zeros(o_ref.shape[1:], F32)
    half = FF // 2
    for f in range(2):
        cols = slice(f * half, (f + 1) * half)
        h1 = _mm(x, w1_ref[0, :, cols])
        h3 = _mm(x, w3_ref[0, :, cols])
        acc = acc + _mm((_silu(h1) * h3).astype(BF16), w2_ref[0, cols, :])
    o_ref[0] = acc.astype(BF16)


def _ffn(xe, w1, w3, w2):
    n_slots = xe.shape[1]
    tm = 512
    while n_slots % tm:
        tm //= 2
    return pl.pallas_call(
        _ffn_kernel,
        grid=(NE, n_slots // tm),
        in_specs=[pl.BlockSpec((1, tm, D), lambda e, m: (e, m, 0)),
                  pl.BlockSpec((1, D, FF), lambda e, m: (e, 0, 0)),
                  pl.BlockSpec((1, D, FF), lambda e, m: (e, 0, 0)),
                  pl.BlockSpec((1, FF, D), lambda e, m: (e, 0, 0))],
        out_specs=pl.BlockSpec((1, tm, D), lambda e, m: (e, m, 0)),
        out_shape=jax.ShapeDtypeStruct((NE, n_slots, D), BF16),
        compiler_params=_cp(("parallel", "parallel")),
        name="moe_ffn",
    )(xe, w1, w3, w2)


def _combine_kernel(cap, x_ref, ye_ref, rank_ref, gate_ref, mod_ref, o_ref):
    T = x_ref.shape[0]
    lane = lax.broadcasted_iota(jnp.int32, (T, cap), 1).astype(F32)
    acc = jnp.zeros(x_ref.shape, F32)
    for e in range(NE):
        p = jnp.where(rank_ref[:, e:e + 1] == lane, 1.0, 0.0).astype(BF16)
        acc = acc + gate_ref[:, e:e + 1] * _mm(p, ye_ref[e])
    o_ref[...] = x_ref[...] + mod_ref[0] * acc


def _combine(x, ye, rank_tm, gate_tm, mod3, T, nb, rb0, sb0, nP):
    cap = 2 * T // NE
    M = x.shape[0]
    tn = 256
    per = T // TM
    mod_idx = _mod_index(nP)
    return pl.pallas_call(
        functools.partial(_combine_kernel, cap),
        grid=(nb, D // tn),
        in_specs=[pl.BlockSpec((T, tn), lambda b, j: (rb0 + b, j)),
                  pl.BlockSpec((NE, cap, tn), lambda b, j: (0, sb0 + b, j)),
                  pl.BlockSpec((T, NE), lambda b, j: (rb0 + b, 0)),
                  pl.BlockSpec((T, NE), lambda b, j: (rb0 + b, 0)),
                  pl.BlockSpec((1, 1, tn),
                               lambda b, j: (mod_idx((rb0 + b) * per)[0], 0, 5 * (D // tn) + j))],
        out_specs=pl.BlockSpec((T, tn), lambda b, j: (rb0 + b, j)),
        out_shape=jax.ShapeDtypeStruct((M, D), F32),
        input_output_aliases={0: 0},
        compiler_params=_cp(("parallel", "parallel")),
        name="moe_combine",
    )(x, ye, rank_tm, gate_tm, mod3)


def _final_kernel(x_ref, g_ref, o_ref):
    x = x_ref[...]
    o_ref[...] = x * lax.rsqrt(jnp.mean(x * x, axis=-1, keepdims=True) + NORM_EPS) * g_ref[...]


def _final_norm(x, g):
    M = x.shape[0]
    return pl.pallas_call(
        _final_kernel,
        grid=(M // 512,),
        in_specs=[pl.BlockSpec((512, D), lambda i: (i, 0)), pl.BlockSpec((1, D), lambda i: (0, 0))],
        out_specs=pl.BlockSpec((512, D), lambda i: (i, 0)),
        out_shape=jax.ShapeDtypeStruct((M, D), F32),
        compiler_params=_cp(("parallel",)),
        name="final_norm",
    )(x, g)


def _rope_lane_tables(T):
    rows = T // GRID_W
    row = jnp.repeat(jnp.arange(rows, dtype=F32), GRID_W)
    col = jnp.tile(jnp.arange(GRID_W, dtype=F32), rows)
    nf = HD // 4
    inv = ROPE_BASE ** (-jnp.arange(nf, dtype=F32) / nf)
    ang = jnp.concatenate([row[:, None] * inv, col[:, None] * inv], axis=-1)
    cos, sin = jnp.cos(ang), jnp.sin(ang)
    return (jnp.tile(jnp.concatenate([cos, cos], axis=-1), (1, NH)),
            jnp.tile(jnp.concatenate([-sin, sin], axis=-1), (1, NH)))


def kernel(x_prompt, x_sample, state_rwkv, state_ret, state_hgrn, c, c_ctx, ada_w, ada_b, norm1_g, norm2_g, final_g, w_in, rw_mu, rw_w0, rw_w_up, rw_a0, rw_a_up, rw_g_up, rw_kvec, rw_ln, hy_conv, hy_ffn1, hy_ffn1_b, hy_ffn2, hy_ffn2_b, hy_ffn3, hy_freq, hy_decay, hy_skip, ret_rate, ret_gn, hg_lb, hg_norm, br_w, w_out, router, ex_w1, ex_w3, ex_w2):
    nP, nS, L = x_prompt.shape[0], x_sample.shape[0], w_in.shape[0]
    assert x_prompt.shape[1:] == (TP, D) and x_sample.shape[1:] == (TS, D)
    assert nS <= 8 and (nP * TP) % TS == 0
    MP = nP * TP
    M = MP + nS * TS
    sblk = MP // TS
    x = jnp.concatenate([x_prompt.reshape(MP, D), x_sample.reshape(nS * TS, D)], axis=0)

    cv = jnp.zeros((16, D), F32).at[:nS].set(c).at[8].set(c_ctx)
    mod = _ada(cv, ada_w, ada_b)

    lb_p = jax.nn.softmax(hg_lb.astype(F32), axis=0)
    lower_bounds = jnp.cumsum(lb_p, axis=0) - lb_p[0]

    bd = (jnp.arange(BW)[:, None] // HD == jnp.arange(BW)[None, :] // HD).astype(BF16)
    tri = _hg_tri()
    ut = {T: (jnp.arange(T)[:, None] <= jnp.arange(T)[None, :]).astype(BF16) for T in (TP, TS)}
    tabs = {T: _dft_tables(T) for T in (TP, TS)}
    feats = {T: _hy_feats(T) for T in (TP, TS)}
    cosT, sinT = _rope_lane_tables(TS)
    ones_p = jnp.ones((TP, BW), F32)
    zero_state = jnp.zeros((nP, 2, NH, HD, HD), F32)
    groups = ((TP, nP, 0, 0), (TS, nS, sblk, nP * (2 * TP // NE) // (2 * TS // NE)))
    n_slots = nP * (2 * TP // NE) + nS * (2 * TS // NE)

    rw_states, ret_states, hg_states = [], [], []
    for l in range(L):
        mod3 = mod[l].reshape(16, 1, 6 * D)
        wl = w_in[l]
        o1 = RW_COLS
        o2 = o1 + HY_COLS
        o3 = o2 + RET_COLS
        o4 = o3 + HG_COLS
        hb, z_rw, z_hy, z_ret, z_hg = _inproj(
            x, norm1_g[l][None], mod3, wl[:, :o1].astype(BF16), wl[:, o1:o2].astype(BF16),
            wl[:, o2:o3].astype(BF16), wl[:, o3:o4].astype(BF16), nP)

        wlr = jnp.zeros((128, 4 * BW), F32)
        wlr = wlr.at[0:32, 0:BW].set(rw_w_up[l, 0]).at[0:32, BW:2 * BW].set(rw_w_up[l, 1])
        wlr = wlr.at[32:64, 2 * BW:3 * BW].set(rw_a_up[l]).at[64:128, 3 * BW:].set(rw_g_up[l])
        vec = jnp.concatenate([rw_w0[l], rw_a0[l][None], rw_kvec[l], jnp.zeros((2, BW), F32)], axis=0)
        prep_p = _prep(z_rw, z_hy, rw_mu[l], hy_conv[l], wlr, vec, bd, 0, nP, TP)
        prep_s = _prep(z_rw, z_hy, rw_mu[l], hy_conv[l], wlr, vec, bd, MP // TM, nS, TS, prep_p[7:])
        bonus, g_rw, hx = prep_s[7:]

        yf_p, yb_p, s_rw_p = _rwkv_scan_group(prep_p[:7], zero_state, nP, TP)
        yf_s, yb_s, _ = _rwkv_scan_group(prep_s[:7], state_rwkv[:, l], nS, TS)
        rw_states.append(s_rw_p)

        f1p = jnp.pad(hy_ffn1[l], ((0, 128 - hy_ffn1.shape[1]), (0, 0)))
        u1 = y_hy = None
        specs = {}
        for T, nb, rb0, _ in groups:
            specs[T] = _hy_spectrum(T, feats[T], tabs[T], f1p, hy_ffn1_b[l][None], hy_ffn2[l],
                                    hy_ffn2_b[l][None], hy_ffn3[l], hy_freq[l], hy_decay[l][None])
            u1 = _hy_conv(T, nb, rb0, hx, 0, hx, 1, hy_skip[l], 0, specs[T], tabs[T], u1)
        for T, nb, rb0, _ in groups:
            y_hy = _hy_conv(T, nb, rb0, u1, 0, hx, 2, hy_skip[l], 1, specs[T], tabs[T], y_hy)

        y_ret, s_ret_p = _retention(TP, nP, 0, z_ret, ones_p, ones_p, ret_rate[l].repeat(HD, axis=-1),
                                    ret_gn[l], zero_state, bd, False)
        y_ret, _ = _retention(TS, nS, sblk, z_ret, cosT, sinT, ret_rate[l].repeat(HD, axis=-1),
                              ret_gn[l], state_ret[:, l], bd, True, y_ret)
        ret_states.append(s_ret_p)
        y_hg, s_hg_p = _hgrn(TP, nP, 0, z_hg, lower_bounds[l], hg_norm[l][None], zero_state, bd, tri)
        y_hg, _ = _hgrn(TS, nS, sblk, z_hg, lower_bounds[l], hg_norm[l][None],
                        jnp.swapaxes(state_hgrn[:, l], -1, -2), bd, tri, y_hg)
        hg_states.append(jnp.swapaxes(s_hg_p, -1, -2))

        x, h2b, affT = _merge(x, hb, (yf_p, yb_p), (yf_s, yb_s), bonus, g_rw, y_hy, y_ret, y_hg,
                              wl[:, o4:].astype(BF16), br_w[l].astype(BF16), w_out[l].astype(BF16),
                              rw_ln[l], mod3, norm2_g[l][None], router[l].T, bd, nP)

        rank = xe = None
        for T, nb, rb0, sb0 in groups:
            rank = _topk(affT, T, nb, rb0, ut[T], rank)
        for T, nb, rb0, sb0 in groups:
            xe = _gather(rank, h2b, T, nb, rb0, sb0, n_slots, xe)
        ye = _ffn(xe, ex_w1[l].astype(BF16), ex_w3[l].astype(BF16), ex_w2[l].astype(BF16))
        rank_tm, gate_tm = rank.T, affT.T
        for T, nb, rb0, sb0 in groups:
            x = _combine(x, ye, rank_tm, gate_tm, mod3, T, nb, rb0, sb0, nP)

    y = _final_norm(x, final_g[None])
    pack = lambda states: jnp.stack(states, axis=1)
    return (y[:MP].reshape(nP, TP, D), y[MP:].reshape(nS, TS, D),
            pack(rw_states), pack(ret_states), pack(hg_states))
```

```python
import functools
import math

import jax
import jax.numpy as jnp
from jax import lax
from jax.experimental import pallas as pl
from jax.experimental.pallas import tpu as pltpu

F32 = jnp.float32
BF16 = jnp.bfloat16

D = 1024
BW = 256
NH = 4
HD = 64
TP = 256
TS = 2048
TM = 256
GRID_W = 64
RW_COLS = 3 * BW + 128
HY_COLS = 3 * BW
RET_COLS = 4 * BW
HG_COLS = 5 * BW
MG_COLS = 4 * D
NE = 16
FF = 2 * D
RET_CHUNK = 128
HG_CHUNK = 16
HG_GROUP = 8
HY_BANDS = 16
ROPE_BASE = 10000.0
NORM_EPS = 1e-6
RW_GN_EPS = 64e-5
GN_EPS = 1e-5
LB_FLOOR = 1e-30
SCAN_LANES = 128
SCAN_TC = 32
SCAN_VB = 32
SCAN_KU = 32
SCAN_ACC = 4
RL_TT = 16
RL_UNROLL = 4
FCH = 256
VMEM_LIMIT = 56 * 1024 * 1024


def _cp(sem, vmem=VMEM_LIMIT):
    return pltpu.CompilerParams(dimension_semantics=sem, vmem_limit_bytes=vmem)


def _sigmoid(x):
    return 1.0 / (1.0 + jnp.exp(-x))


def _silu(x):
    return x * _sigmoid(x)


def _softplus(x):
    return jnp.maximum(x, 0.0) + jnp.log(1.0 + jnp.exp(-jnp.abs(x)))


def _split(a):
    hi = a.astype(BF16)
    lo = (a - hi.astype(F32)).astype(BF16)
    return hi, lo


_NN = (((1,), (0,)), ((), ()))
_NT = (((1,), (1,)), ((), ()))
_TN = (((0,), (0,)), ((), ()))


def _mm(a, b, dims=_NN):
    return lax.dot_general(a, b, dims, preferred_element_type=F32)


def _dot3(a, b, dims=_NN):
    ah, al = _split(a)
    bh, bl = _split(b)
    return _mm(ah, bh, dims) + (_mm(ah, bl, dims) + _mm(al, bh, dims))


def _dot2l(a, b16, dims=_NN):
    ah, al = _split(a)
    return _mm(ah, b16, dims) + _mm(al, b16, dims)


def _dot2r(a16, b, dims=_NN):
    bh, bl = _split(b)
    return _mm(a16, bh, dims) + _mm(a16, bl, dims)


def _mm3(th, tl, xh, xl):
    return _mm(th, xh) + (_mm(th, xl) + _mm(tl, xh))


def _drop_ref(kern, idx):
    def wrapped(*refs):
        return kern(*refs[:idx], *refs[idx + 1:])
    return wrapped


def _ada_kernel(cv_ref, w_ref, b_ref, o_ref):
    cv = cv_ref[...]
    o_ref[0] = _dot3(_silu(cv), w_ref[0]) + b_ref[0]


def _ada(cv, ada_w, ada_b):
    L = ada_w.shape[0]
    tn = 1536
    return pl.pallas_call(
        _ada_kernel,
        grid=(L, 6 * D // tn),
        in_specs=[pl.BlockSpec((16, D), lambda l, j: (0, 0)),
                  pl.BlockSpec((1, D, tn), lambda l, j: (l, 0, j)),
                  pl.BlockSpec((1, 1, tn), lambda l, j: (l, 0, j))],
        out_specs=pl.BlockSpec((1, 16, tn), lambda l, j: (l, 0, j)),
        out_shape=jax.ShapeDtypeStruct((L, 16, 6 * D), F32),
        compiler_params=_cp(("parallel", "parallel")),
        name="ada",
    )(cv, ada_w, ada_b.reshape(L, 1, 6 * D))


def _mod_index(nP):
    def idx(i):
        return (jnp.where(i < nP, 8, (i - nP) // (TS // TM)), 0, 0)
    return idx


def _inproj_kernel(x_ref, g_ref, mod_ref, wrw_ref, why_ref, wret_ref, whg_ref,
                   hb_ref, zrw_ref, zhy_ref, zret_ref, zhg_ref):
    x = x_ref[...]
    y = x * lax.rsqrt(jnp.mean(x * x, axis=-1, keepdims=True) + NORM_EPS) * g_ref[...]
    mod = mod_ref[0]
    h = y * (1.0 + mod[:, D:2 * D]) + mod[:, 0:D]
    hb = h.astype(BF16)
    hb_ref[...] = hb
    zrw_ref[...] = _mm(hb, wrw_ref[...])
    zhy_ref[...] = _mm(hb, why_ref[...])
    zret_ref[...] = _mm(hb, wret_ref[...])
    zhg_ref[...] = _mm(hb, whg_ref[...])


def _inproj(x, g, mod3, w_rw, w_hy, w_ret, w_hg, nP):
    M = x.shape[0]
    row = lambda i: (i, 0)
    const = lambda i: (0, 0)
    widths = (RW_COLS, HY_COLS, RET_COLS, HG_COLS)
    return pl.pallas_call(
        _inproj_kernel,
        grid=(M // TM,),
        in_specs=[pl.BlockSpec((TM, D), row),
                  pl.BlockSpec((1, D), const),
                  pl.BlockSpec((1, 1, 6 * D), _mod_index(nP))]
                 + [pl.BlockSpec((D, w), const) for w in widths],
        out_specs=[pl.BlockSpec((TM, D), row)] + [pl.BlockSpec((TM, w), row) for w in widths],
        out_shape=[jax.ShapeDtypeStruct((M, D), BF16)]
                  + [jax.ShapeDtypeStruct((M, w), F32) for w in widths],
        compiler_params=_cp(("parallel",)),
        name="inproj",
    )(x, g, mod3, w_rw, w_hy, w_ret, w_hg)


def _shift_prev(z, halo8, first):
    rolled = pltpu.roll(z, 1, 0)
    halo = jnp.where(first, 0.0, halo8[7:8, :])
    row = lax.broadcasted_iota(jnp.int32, z.shape, 0)
    return jnp.where(row == 0, halo, rolled)


def _shift_next(z, halo8, last):
    n = z.shape[0]
    rolled = pltpu.roll(z, n - 1, 0)
    halo = jnp.where(last, 0.0, halo8[0:1, :])
    row = lax.broadcasted_iota(jnp.int32, z.shape, 0)
    return jnp.where(row == n - 1, halo, rolled)


def _prep_kernel(tiles, zrw_ref, zrw_p_ref, zrw_n_ref, zhy_ref, zhy_p_ref, zhy_n_ref,
                 mu_ref, hc_ref, wlr_ref, vec_ref, bd_ref, *rest):
    r_ref, wf_ref, wb_ref, kk_ref, kka_ref, k_ref, v_ref, bonus_ref, g_ref, hx_ref = rest[-10:]
    i = pl.program_id(0)
    first = i % tiles == 0
    last = i % tiles == tiles - 1

    def put(ref, val):
        ref[0] = val[:, 0:128]
        ref[1] = val[:, 128:256]

    z = zhy_ref[...]
    hc = hc_ref[...]
    hx_ref[...] = (hc[0:1] * _shift_prev(z, zhy_p_ref[...], first) + hc[1:2] * z
                   + hc[2:3] * _shift_next(z, zhy_n_ref[...], last))

    z = zrw_ref[...]
    mu = mu_ref[...]
    xr = (mu[0:1] * _shift_prev(z, zrw_p_ref[...], first) + (1.0 - mu[0:1] - mu[1:2]) * z
          + mu[1:2] * _shift_next(z, zrw_n_ref[...], last))
    r = xr[:, 0:BW]
    k = xr[:, BW:2 * BW]
    v = xr[:, 2 * BW:3 * BW]
    xl = xr[:, 3 * BW:3 * BW + 128]
    lane = lax.broadcasted_iota(jnp.int32, xl.shape, 1)
    f = jnp.where(lane < 32, jnp.tanh(xl), jnp.where(lane < 64, xl, _sigmoid(xl)))
    lr = _dot3(f, wlr_ref[...])
    vec = vec_ref[...]
    bd = bd_ref[...]
    for d, out in ((0, wf_ref), (1, wb_ref)):
        pre = vec[d:d + 1] + lr[:, d * BW:(d + 1) * BW]
        wlog = -_softplus(-pre) - 0.5
        put(out, jnp.exp(-jnp.exp(wlog)))
    a = _sigmoid(vec[2:3] + lr[:, 2 * BW:3 * BW])
    g_ref[...] = lr[:, 3 * BW:4 * BW]
    kkr = k * vec[3:4]
    nrm = jnp.sqrt(_dot2l(kkr * kkr, bd))
    kk = kkr / jnp.maximum(nrm, 1e-12)
    k2 = k * (1.0 + (a - 1.0) * vec[4:5])
    put(r_ref, r)
    put(kk_ref, kk)
    put(kka_ref, kk * a)
    put(k_ref, k2)
    put(v_ref, v)
    bonus_ref[...] = _dot2l(r * k2 * vec[5:6], bd) * v


def _prep(z_rw, z_hy, mu, hc, wlr, vec, bd, tile0, nb, T, prev=None):
    M = z_rw.shape[0]
    nb8 = M // 8
    r8 = TM // 8
    ntile = nb * T // TM
    row = lambda i: (tile0 + i, 0)
    prv = lambda i: (jnp.maximum((tile0 + i) * r8 - 1, 0), 0)
    nxt = lambda i: (jnp.minimum((tile0 + i) * r8 + r8, nb8 - 1), 0)
    const = lambda i: (0, 0)
    in_specs = [pl.BlockSpec((TM, RW_COLS), row), pl.BlockSpec((8, RW_COLS), prv),
                pl.BlockSpec((8, RW_COLS), nxt),
                pl.BlockSpec((TM, HY_COLS), row), pl.BlockSpec((8, HY_COLS), prv),
                pl.BlockSpec((8, HY_COLS), nxt),
                pl.BlockSpec((2, RW_COLS), const), pl.BlockSpec((3, HY_COLS), const),
                pl.BlockSpec((128, 4 * BW), const), pl.BlockSpec((8, BW), const),
                pl.BlockSpec((BW, BW), const)]
    args = [z_rw, z_rw, z_rw, z_hy, z_hy, z_hy, mu, hc, wlr, vec, bd]
    aliases = {}
    if prev is not None:
        for j, a in enumerate(prev):
            in_specs.append(pl.BlockSpec(memory_space=pl.ANY))
            aliases[len(args)] = 7 + j
            args.append(a)
    half = pl.BlockSpec((2, TM, 128), lambda i: (0, i, 0))
    return pl.pallas_call(
        functools.partial(_prep_kernel, T // TM),
        grid=(ntile,),
        in_specs=in_specs,
        out_specs=[half] * 7 + [pl.BlockSpec((TM, BW), row)] * 2 + [pl.BlockSpec((TM, HY_COLS), row)],
        out_shape=[jax.ShapeDtypeStruct((2, ntile * TM, 128), F32)] * 7
                  + [jax.ShapeDtypeStruct((M, BW), F32)] * 2 + [jax.ShapeDtypeStruct((M, HY_COLS), F32)],
        input_output_aliases=aliases,
        compiler_params=_cp(("parallel",)),
        name="prep",
    )(*args)


def _rl_in_kernel(G, tt, r_ref, kk_ref, kka_ref, k_ref, wf_ref, wb_ref, v_ref,
                  ro_ref, kko_ref, kkao_ref, ko_ref, wo_ref, vo_ref):
    vp = HD // G

    def head_pieces(ref, tl, lo, width):
        halves = [ref[hh, :, tl, :] for hh in range(2)]
        return [halves[h // 2][:, (h % 2) * HD + lo:(h % 2) * HD + lo + width] for h in range(NH)]

    def body(tl, carry):
        def rows(ref):
            return jnp.concatenate(head_pieces(ref, tl, 0, HD) * G, axis=0).T
        ro_ref[tl] = rows(r_ref)
        kko_ref[tl] = rows(kk_ref)
        kkao_ref[tl] = rows(kka_ref)
        ko_ref[tl] = rows(k_ref)
        wo_ref[0, tl] = rows(wf_ref)
        wo_ref[1, tl] = rows(wb_ref)
        pieces = []
        for g in range(G):
            pieces += head_pieces(v_ref, tl, g * vp, vp)
        vo_ref[tl] = jnp.concatenate(pieces, axis=0).T
        return carry

    lax.fori_loop(0, tt, body, 0, unroll=RL_UNROLL)


def _rl_in(parts, nb, T, G):
    vp = HD // G
    tt = RL_TT
    src = pl.BlockSpec((2, nb, tt, 128), lambda i: (0, 0, i, 0))
    rows = pl.BlockSpec((tt, HD, SCAN_LANES), lambda i: (i, 0, 0))
    return pl.pallas_call(
        functools.partial(_rl_in_kernel, G, tt),
        grid=(T // tt,),
        in_specs=[src] * 7,
        out_specs=[rows] * 4 + [pl.BlockSpec((2, tt, HD, SCAN_LANES), lambda i: (0, i, 0, 0)),
                                pl.BlockSpec((tt, vp, SCAN_LANES), lambda i: (i, 0, 0))],
        out_shape=[jax.ShapeDtypeStruct((T, HD, SCAN_LANES), F32)] * 4
                  + [jax.ShapeDtypeStruct((2, T, HD, SCAN_LANES), F32),
                     jax.ShapeDtypeStruct((T, vp, SCAN_LANES), F32)],
        compiler_params=_cp(("parallel",)),
        name="rwkv_layout_in",
    )(*[a.reshape(2, nb, T, 128) for a in parts])


def _rl_out_kernel(G, nb, tt, y_ref, yf_ref, yb_ref):
    def body(tl, carry):
        for d, out in ((0, yf_ref), (1, yb_ref)):
            z = y_ref[d, tl].T
            for hh in range(2):
                out[hh, :, tl, :] = jnp.concatenate(
                    [z[(g * NH + h) * nb:(g * NH + h + 1) * nb, :] for h in (2 * hh, 2 * hh + 1) for g in range(G)],
                    axis=1)
        return carry

    lax.fori_loop(0, tt, body, 0, unroll=RL_UNROLL)


def _rl_out(y, nb, T, G):
    vp = HD // G
    tt = RL_TT
    dst = pl.BlockSpec((2, nb, tt, 128), lambda i: (0, 0, i, 0))
    yf, yb = pl.pallas_call(
        functools.partial(_rl_out_kernel, G, nb, tt),
        grid=(T // tt,),
        in_specs=[pl.BlockSpec((2, tt, vp, SCAN_LANES), lambda i: (0, i, 0, 0))],
        out_specs=[dst, dst],
        out_shape=[jax.ShapeDtypeStruct((2, nb, T, 128), F32)] * 2,
        compiler_params=_cp(("parallel",)),
        name="rwkv_layout_out",
    )(y)
    return yf.reshape(2, nb * T, 128), yb.reshape(2, nb * T, 128)


def _scan_kernel(nc, r_ref, w_ref, kk_ref, kka_ref, k_ref, v_ref, s0_ref, y_ref, sT_ref, s_scr):
    d = pl.program_id(0)
    c = pl.program_id(1)

    @pl.when(c == 0)
    def _():
        s_scr[...] = s0_ref[0]

    vp = v_ref.shape[1]
    vb = min(vp, SCAN_VB)

    def step(i, carry):
        t = i + d * (SCAN_TC - 1 - 2 * i)
        for v0 in range(0, vp, vb):
            vs = slice(v0, v0 + vb)
            vt = v_ref[t, vs, :]

            zero = jnp.zeros((vb, SCAN_LANES), F32)
            acc0 = (zero,) * SCAN_ACC

            def dot_kk(j, acc):
                out = []
                for a in range(SCAN_ACC):
                    kc = j * SCAN_ACC + a
                    out.append(acc[a] + s_scr[kc, vs, :] * kk_ref[t, pl.ds(kc, 1), :])
                return tuple(out)

            sa = sum(lax.fori_loop(0, HD // SCAN_ACC, dot_kk, acc0, unroll=SCAN_KU // SCAN_ACC))

            def update(j, acc):
                out = []
                for a in range(SCAN_ACC):
                    kc = j * SCAN_ACC + a
                    row = pl.ds(kc, 1)
                    s_new = (s_scr[kc, vs, :] * w_ref[0, t, row, :]
                             + (vt * k_ref[t, row, :] - sa * kka_ref[t, row, :]))
                    s_scr[kc, vs, :] = s_new
                    out.append(acc[a] + s_new * r_ref[t, row, :])
                return tuple(out)

            y_ref[0, t, vs, :] = sum(lax.fori_loop(0, HD // SCAN_ACC, update, acc0,
                                                   unroll=SCAN_KU // SCAN_ACC))
        return carry

    lax.fori_loop(0, SCAN_TC, step, 0)

    @pl.when(c == nc - 1)
    def _():
        sT_ref[0] = s_scr[...]


def _scan(r, w, kk, kka, k, v, s0):
    T = r.shape[0]
    vp = v.shape[1]
    nc = T // SCAN_TC
    tb = lambda d, c: c + d * (nc - 1 - 2 * c)
    rows = pl.BlockSpec((SCAN_TC, HD, SCAN_LANES), lambda d, c: (tb(d, c), 0, 0))
    wrows = pl.BlockSpec((1, SCAN_TC, HD, SCAN_LANES), lambda d, c: (d, tb(d, c), 0, 0))
    vrows = pl.BlockSpec((SCAN_TC, vp, SCAN_LANES), lambda d, c: (tb(d, c), 0, 0))
    yrows = pl.BlockSpec((1, SCAN_TC, vp, SCAN_LANES), lambda d, c: (d, tb(d, c), 0, 0))
    st = pl.BlockSpec((1, HD, vp, SCAN_LANES), lambda d, c: (d, 0, 0, 0))
    return pl.pallas_call(
        functools.partial(_scan_kernel, nc),
        grid=(2, nc),
        in_specs=[rows, wrows, rows, rows, rows, vrows, st],
        out_specs=[yrows, st],
        out_shape=[jax.ShapeDtypeStruct((2, T, vp, SCAN_LANES), F32),
                   jax.ShapeDtypeStruct((2, HD, vp, SCAN_LANES), F32)],
        scratch_shapes=[pltpu.VMEM((HD, vp, SCAN_LANES), F32)],
        compiler_params=_cp(("parallel", "arbitrary")),
        name="rwkv_scan",
    )(r, w, kk, kka, k, v, s0)


def _to_scan_state(s, nb, G):
    vp = HD // G
    a = s.reshape(nb, 2, NH, G, vp, HD)
    return a.transpose(1, 5, 4, 3, 2, 0).reshape(2, HD, vp, G * NH * nb)


def _from_scan_state(s, nb, G):
    vp = HD // G
    a = s.reshape(2, HD, vp, G, NH, nb).transpose(5, 0, 4, 3, 2, 1)
    return a.reshape(nb, 2, NH, HD, HD)


def _rwkv_scan_group(parts, s0, nb, T):
    r, wf, wb, kk, kka, k, v = parts
    assert SCAN_LANES % (nb * NH) == 0
    G = SCAN_LANES // (nb * NH)
    r_s, kk_s, kka_s, k_s, w_s, v_s = _rl_in((r, kk, kka, k, wf, wb, v), nb, T, G)
    y, sT = _scan(r_s, w_s, kk_s, kka_s, k_s, v_s, _to_scan_state(s0, nb, G))
    y_fw, y_bw = _rl_out(y, nb, T, G)
    return y_fw, y_bw, _from_scan_state(sT, nb, G)


def _dft_tables(T):
    N = 2 * T
    i = jnp.arange(T, dtype=jnp.int32)
    prod = ((2 * i[:, None] + 1) * (2 * i[None, :] + 1)) % (4 * N)
    ang = prod.astype(F32) * (2.0 * math.pi / (4 * N))
    half = (2 * i + 1).astype(F32) * (math.pi / (2 * N))
    ph = jnp.zeros((T, 128), F32).at[:, 0].set(jnp.cos(half)).at[:, 1].set(jnp.sin(half))
    return _split(jnp.cos(ang)) + _split(jnp.sin(ang)) + (ph,)


def _hy_feats(T):
    t = jnp.linspace(0.0, 1.0, T, dtype=F32)[:, None]
    w = (2.0 * math.pi / T) * jnp.arange(T, dtype=F32)[:, None]
    bands = jnp.linspace(1e-4, HY_BANDS - 1.0, HY_BANDS, dtype=F32)[None, :]
    feats = jnp.concatenate([t, jnp.cos(bands * w), -jnp.sin(bands * w)], axis=-1)
    return jnp.pad(feats, ((0, 0), (0, 128 - feats.shape[1])))


def _spec_kernel(T, feats_ref, f1_ref, b1_ref, f2_ref, b2_ref, f3_ref, freq_ref, dec_ref,
                 ch_ref, cl_ref, sh_ref, sl_ref, ph_ref, re_ref, im_ref, fh_scr, fl_scr):
    j = pl.program_id(0)

    @pl.when(j == 0)
    def _():
        feats = feats_ref[...]
        freq = freq_ref[...]
        h1 = jnp.sin(freq[0:1] * (_dot3(feats, f1_ref[...]) + b1_ref[...]))
        h2 = jnp.sin(freq[1:2] * (_dot3(h1, f2_ref[...]) + b2_ref[...]))
        filt = _dot3(h2, f3_ref[...]) * jnp.exp(-feats[:, 0:1] * jnp.abs(dec_ref[...]))
        row = lax.broadcasted_iota(jnp.int32, (T, BW), 0)
        for o in range(2):
            fw = filt[:, o * 2 * BW:o * 2 * BW + BW]
            bw = filt[:, o * 2 * BW + BW:(o + 1) * 2 * BW]
            den = jnp.sum(jnp.abs(fw) + jnp.abs(bw), axis=0, keepdims=True)
            fw = fw / den
            bw0 = jnp.where(row == 0, 0.0, bw / den)
            for col, val in ((o * BW, fw + bw0), (2 * BW + o * BW, fw - bw0)):
                hi, lo = _split(val)
                fh_scr[:, col:col + BW] = hi
                fl_scr[:, col:col + BW] = lo

    fh = fh_scr[...]
    fl = fl_scr[...]
    c = _mm3(ch_ref[...], cl_ref[...], fh, fl)
    s = _mm3(sh_ref[...], sl_ref[...], fh, fl)
    pc = ph_ref[:, 0:1] * (1.0 / T)
    ps = ph_ref[:, 1:2] * (1.0 / T)
    re_ref[...] = pc * c[:, 0:2 * BW] + ps * s[:, 0:2 * BW]
    im_ref[...] = ps * c[:, 2 * BW:4 * BW] - pc * s[:, 2 * BW:4 * BW]


def _hy_spectrum(T, feats, tabs, f1p, b1, f2, b2, f3, freq, dec):
    ch, cl, sh, sl, ph = tabs
    fch = min(FCH, T)
    const = lambda j: (0, 0)
    rowc = pl.BlockSpec((fch, T), lambda j: (j, 0))
    return pl.pallas_call(
        functools.partial(_spec_kernel, T),
        grid=(T // fch,),
        in_specs=[pl.BlockSpec((T, 128), const), pl.BlockSpec((128, 64), const),
                  pl.BlockSpec((1, 64), const), pl.BlockSpec((64, 64), const),
                  pl.BlockSpec((1, 64), const), pl.BlockSpec((64, 4 * BW), const),
                  pl.BlockSpec((2, 64), const), pl.BlockSpec((1, 4 * BW), const),
                  rowc, rowc, rowc, rowc, pl.BlockSpec((fch, 128), lambda j: (j, 0))],
        out_specs=[pl.BlockSpec((fch, 2 * BW), lambda j: (j, 0))] * 2,
        out_shape=[jax.ShapeDtypeStruct((T, 2 * BW), F32)] * 2,
        scratch_shapes=[pltpu.VMEM((T, 4 * BW), BF16), pltpu.VMEM((T, 4 * BW), BF16)],
        compiler_params=_cp(("arbitrary",)),
        name="hy_spectrum",
    )(feats, f1p, b1, f2, b2, f3, freq, dec, ch, cl, sh, sl, ph)


def _conv_kernel(nj, u_ref, x_ref, skip_ref, sre_ref, sim_ref,
                 cr_ref, sr_ref, cc_ref, sc_ref, *rest):
    o_ref, u_scr, acc_scr = rest[-3:]
    j = pl.program_id(1)

    @pl.when(j == 0)
    def _():
        u_scr[...] = u_ref[...].astype(BF16)
        acc_scr[...] = jnp.zeros_like(acc_scr)

    ub = u_scr[...]
    uc = _mm(cr_ref[...], ub)
    us = _mm(sr_ref[...], ub)
    sre = sre_ref[...]
    sim = sim_ref[...]
    yre = (uc * sre + us * sim).astype(BF16)
    yim = (uc * sim - us * sre).astype(BF16)
    acc_scr[...] += _mm(cc_ref[...], yre) - _mm(sc_ref[...], yim)

    @pl.when(j == nj - 1)
    def _():
        u = u_ref[...]
        o_ref[...] = x_ref[...] * (acc_scr[...] + skip_ref[...] * u)


def _hy_conv(T, nb, rb0, u_arr, cu, hx, cx, skip, order, spec, tabs, prev_out=None):
    ch, cl, sh, sl, _ = tabs
    sre, sim = spec
    fch = min(FCH, T)
    nj = T // fch
    M = hx.shape[0]
    rowc = pl.BlockSpec((fch, T), lambda b, j: (j, 0))
    colc = pl.BlockSpec((T, fch), lambda b, j: (0, j))
    sp = pl.BlockSpec((fch, BW), lambda b, j: (j, order))
    in_specs = [pl.BlockSpec((T, BW), lambda b, j: (rb0 + b, cu)),
                pl.BlockSpec((T, BW), lambda b, j: (rb0 + b, cx)),
                pl.BlockSpec((1, BW), lambda b, j: (0, 0)),
                sp, sp, rowc, rowc, colc, colc]
    args = [u_arr, hx, skip[order][None], sre, sim, ch, sh, ch, sh]
    aliases = {}
    if prev_out is not None:
        in_specs.append(pl.BlockSpec(memory_space=pl.ANY))
        args.append(prev_out)
        aliases = {len(args) - 1: 0}
    return pl.pallas_call(
        functools.partial(_conv_kernel, nj),
        grid=(nb, nj),
        in_specs=in_specs,
        out_specs=pl.BlockSpec((T, BW), lambda b, j: (rb0 + b, 0)),
        out_shape=jax.ShapeDtypeStruct((M, BW), F32),
        scratch_shapes=[pltpu.VMEM((T, BW), BF16), pltpu.VMEM((T, BW), F32)],
        input_output_aliases=aliases,
        compiler_params=_cp(("parallel", "arbitrary")),
        name="hy_conv",
    )(*args)


def _head_ln(y, bd, gain, bias, eps):
    mu = _dot2l(y, bd) * (1.0 / HD)
    yc = y - mu
    var = _dot2l(yc * yc, bd) * (1.0 / HD)
    return yc * lax.rsqrt(var + eps) * gain + bias


def _ret_kernel(T, use_rope, z_ref, cos_ref, sin_ref, rate_ref, gn_ref, s0_ref, bd_ref,
                y_ref, sT_ref, q_scr, k_scr, y_scr, s_scr, dm_scr):
    C = RET_CHUNK
    nc = T // C
    q = z_ref[:, 0:BW]
    k = z_ref[:, BW:2 * BW]
    if use_rope:
        lane = lax.broadcasted_iota(jnp.int32, (T, BW), 1) % HD
        cos = cos_ref[...]
        sin = sin_ref[...]

        def rope(x):
            sw = jnp.where(lane < HD // 2, pltpu.roll(x, BW - HD // 2, 1), pltpu.roll(x, HD // 2, 1))
            return x * cos + sw * sin
        q = rope(q)
        k = rope(k)
    q_scr[...] = q * (HD ** -0.5)
    k_scr[...] = k
    s_scr[...] = s0_ref[0]

    lg = -jnp.exp(rate_ref[...])
    n_col = lax.broadcasted_iota(jnp.int32, (C, BW), 0).astype(F32)
    n_r = lax.broadcasted_iota(jnp.int32, (C, C), 0)
    n_c = lax.broadcasted_iota(jnp.int32, (C, C), 1)
    qdec = (jnp.exp((n_col + 1.0) * lg[0:1]), jnp.exp((C - n_col) * lg[1:2]))
    kdec = (jnp.exp((C - 1.0 - n_col) * lg[0:1]), jnp.exp(n_col * lg[1:2]))
    cdec = jnp.exp(float(C) * lg)
    for d in range(2):
        rel = (n_r - n_c) if d == 0 else (n_c - n_r)
        relf = jnp.maximum(rel, 0).astype(F32)
        for h in range(NH):
            lgh = lg[d:d + 1, h * HD:h * HD + 1]
            dm_scr[d * NH + h] = jnp.where(rel >= 0, jnp.exp(relf * lgh), 0.0)

    def both(i, carry):
        units = []
        for d in range(2):
            rows = pl.ds(pl.multiple_of((i if d == 0 else nc - 1 - i) * C, C), C)
            qc = q_scr[rows, :]
            kc = k_scr[rows, :]
            qb = qc.astype(BF16)
            kb = kc.astype(BF16)
            qd = (qc * qdec[d]).astype(BF16)
            kd = (kc * kdec[d]).astype(BF16)
            vb = z_ref[rows, 2 * BW:3 * BW].astype(BF16)
            for h in range(NH):
                hs = slice(h * HD, (h + 1) * HD)
                s = s_scr[d, h]
                units.append((d, h, rows, vb[:, hs], s,
                              _mm(qb[:, hs], kb[:, hs], _NT),
                              _mm(qd[:, hs], s.astype(BF16)),
                              _mm(kd[:, hs], vb[:, hs], _TN)))
        ys = [[], []]
        for d, h, rows, vh, s, att, inter, upd in units:
            att = (att * dm_scr[d * NH + h]).astype(BF16)
            ys[d].append(_mm(att, vh) + inter)
            s_scr[d, h] = s * cdec[d:d + 1, h * HD:h * HD + 1] + upd
        for d in range(2):
            y_scr[d, units[d * NH][2], :] = jnp.concatenate(ys[d], axis=1)
        return carry

    lax.fori_loop(0, nc, both, 0)

    gn = gn_ref[...]
    y = _head_ln(y_scr[0] + y_scr[1], bd_ref[...], gn[0:1], gn[1:2], GN_EPS)
    y_ref[...] = y * _silu(z_ref[:, 3 * BW:4 * BW])
    sT_ref[0] = s_scr[...]


def _retention(T, nb, rb0, z_ret, cosT, sinT, rate_l, gn, s0, bd, use_rope, prev_out=None):
    M = z_ret.shape[0]
    const = lambda b: (0, 0)
    in_specs = [pl.BlockSpec((T, RET_COLS), lambda b: (rb0 + b, 0)),
                pl.BlockSpec((T, BW), const), pl.BlockSpec((T, BW), const),
                pl.BlockSpec((2, BW), const), pl.BlockSpec((2, BW), const),
                pl.BlockSpec((1, 2, NH, HD, HD), lambda b: (b, 0, 0, 0, 0)),
                pl.BlockSpec((BW, BW), const)]
    args = [z_ret, cosT, sinT, rate_l, gn, s0, bd]
    aliases = {}
    if prev_out is not None:
        in_specs.append(pl.BlockSpec(memory_space=pl.ANY))
        args.append(prev_out)
        aliases = {len(args) - 1: 0}
    kern = functools.partial(_ret_kernel, T, use_rope)
    if prev_out is not None:
        kern = _drop_ref(kern, len(args) - 1)
    return pl.pallas_call(
        kern,
        grid=(nb,),
        in_specs=in_specs,
        out_specs=[pl.BlockSpec((T, BW), lambda b: (rb0 + b, 0)),
                   pl.BlockSpec((1, 2, NH, HD, HD), lambda b: (b, 0, 0, 0, 0))],
        out_shape=[jax.ShapeDtypeStruct((M, BW), F32),
                   jax.ShapeDtypeStruct((nb, 2, NH, HD, HD), F32)],
        scratch_shapes=[pltpu.VMEM((T, BW), F32), pltpu.VMEM((T, BW), F32), pltpu.VMEM((2, T, BW), F32),
                        pltpu.VMEM((2, NH, HD, HD), F32), pltpu.VMEM((2 * NH, RET_CHUNK, RET_CHUNK), F32)],
        input_output_aliases=aliases,
        compiler_params=_cp(("parallel",)),
        name="retention",
    )(*args)


def _hg_kernel(T, z_ref, lb_ref, norm_ref, s0_ref, bd_ref, tri_ref,
               y_ref, sT_ref, q_scr, kin_scr, lf_scr, y_scr, st_scr):
    C = HG_CHUNK
    nc = T // C
    q_scr[...] = _silu(z_ref[:, 0:BW])
    for d in range(2):
        zf = z_ref[:, (1 + d) * BW:(2 + d) * BW]
        lb = lb_ref[d:d + 1, :]
        la = jnp.log(jnp.maximum(lb, LB_FLOOR)) + jnp.zeros_like(zf)
        lc = jnp.log(1.0 - lb) - _softplus(-zf)
        lf_scr[d] = jnp.maximum(la, lc) + jnp.log(1.0 + jnp.exp(-jnp.abs(la - lc)))
        kin_scr[d] = (1.0 - lb) * _sigmoid(-zf)
    st_scr[...] = s0_ref[0]
    bd = bd_ref[...]
    row = lax.broadcasted_iota(jnp.int32, (C, BW), 0)
    G = HG_GROUP
    R = G * C
    ng = T // R

    def shifted(a, o, d):
        if o == 0:
            return a
        return pltpu.roll(a, o if d == 0 else C - o, 0)

    def group(i, carry):
        dirs = []
        for d in range(2):
            rows = pl.ds(pl.multiple_of((i if d == 0 else ng - 1 - i) * R, R), R)
            g = lf_scr[d, rows, :]
            g1 = g.astype(BF16)
            g2 = (g - g1.astype(F32)).astype(BF16)
            g3 = (g - g1.astype(F32) - g2.astype(F32)).astype(BF16)
            tri = tri_ref[d]
            b = _mm(tri, g1) + (_mm(tri, g2) + _mm(tri, g3))
            dirs.append((rows, b, q_scr[rows, :], kin_scr[d, rows, :], z_ref[rows, 3 * BW:4 * BW]))

        atts = []
        for d, (rows, b, q, kin, v) in enumerate(dirs):
            ps = []
            for c in range(G):
                cs = slice(c * C, (c + 1) * C)
                bc, qc, kc = b[cs], q[cs], kin[cs]
                for o in range(C):
                    mask = (row >= o) if d == 0 else (row <= C - 1 - o)
                    e = jnp.exp(jnp.minimum(bc - shifted(bc, o, d), 0.0))
                    ps.append(jnp.where(mask, e * shifted(kc, o, d) * qc, 0.0).astype(BF16))
            atts.append(_mm(jnp.concatenate(ps, axis=0), bd))

        upd = []
        for d, (rows, b, q, kin, v) in enumerate(dirs):
            per_chunk = []
            for c in range(G):
                cs = slice(c * C, (c + 1) * C)
                bc = b[cs]
                b_end = bc[C - 1:C, :] if d == 0 else bc[0:1, :]
                kd = (kin[cs] * jnp.exp(b_end - bc)).astype(BF16)
                vb = v[cs].astype(BF16)
                u = [_mm(vb[:, h * HD:(h + 1) * HD], kd[:, h * HD:(h + 1) * HD], _TN) for h in range(NH)]
                per_chunk.append((u, jnp.exp(b_end), (q[cs] * jnp.exp(bc)).astype(BF16)))
            upd.append(per_chunk)

        for d, (rows, b, q, kin, v) in enumerate(dirs):
            att = atts[d]
            ys = [None] * G
            for c in range(G):
                vc = v[c * C:(c + 1) * C]
                y = jnp.zeros((C, BW), F32)
                for o in range(C):
                    y = y + att[(c * C + o) * C:(c * C + o + 1) * C, :] * shifted(vc, o, d)
                ys[c] = y
            st = [st_scr[d, h] for h in range(NH)]
            for c in (range(G) if d == 0 else range(G - 1, -1, -1)):
                u, eb, qe = upd[d][c]
                inter = []
                for h in range(NH):
                    hs = slice(h * HD, (h + 1) * HD)
                    inter.append(_mm(qe[:, hs], st[h].astype(BF16), _NT))
                    st[h] = eb[:, hs] * st[h] + u[h]
                ys[c] = ys[c] + jnp.concatenate(inter, axis=1)
            for h in range(NH):
                st_scr[d, h] = st[h]
            y_scr[d, rows, :] = jnp.concatenate(ys, axis=0)
        return carry

    lax.fori_loop(0, ng, group, 0)

    y = y_scr[0] + y_scr[1]
    ms = _dot2l(y * y, bd) * (1.0 / HD)
    y_ref[...] = y * lax.rsqrt(ms + NORM_EPS) * norm_ref[...] * _silu(z_ref[:, 4 * BW:5 * BW])
    sT_ref[0] = st_scr[...]


def _hg_tri():
    ci = jnp.arange(HG_CHUNK * HG_GROUP)
    same = ci[:, None] // HG_CHUNK == ci[None, :] // HG_CHUNK
    return jnp.stack([same & (ci[:, None] >= ci[None, :]), same & (ci[:, None] <= ci[None, :])]).astype(BF16)


def _hgrn(T, nb, rb0, z_hg, lb, norm, s0t, bd, tri, prev_out=None):
    M = z_hg.shape[0]
    const = lambda b: (0, 0)
    in_specs = [pl.BlockSpec((T, HG_COLS), lambda b: (rb0 + b, 0)),
                pl.BlockSpec((2, BW), const), pl.BlockSpec((1, BW), const),
                pl.BlockSpec((1, 2, NH, HD, HD), lambda b: (b, 0, 0, 0, 0)),
                pl.BlockSpec((BW, BW), const),
                pl.BlockSpec((2, HG_CHUNK * HG_GROUP, HG_CHUNK * HG_GROUP), lambda b: (0, 0, 0))]
    args = [z_hg, lb, norm, s0t, bd, tri]
    aliases = {}
    if prev_out is not None:
        in_specs.append(pl.BlockSpec(memory_space=pl.ANY))
        args.append(prev_out)
        aliases = {len(args) - 1: 0}
    kern = functools.partial(_hg_kernel, T)
    if prev_out is not None:
        kern = _drop_ref(kern, len(args) - 1)
    return pl.pallas_call(
        kern,
        grid=(nb,),
        in_specs=in_specs,
        out_specs=[pl.BlockSpec((T, BW), lambda b: (rb0 + b, 0)),
                   pl.BlockSpec((1, 2, NH, HD, HD), lambda b: (b, 0, 0, 0, 0))],
        out_shape=[jax.ShapeDtypeStruct((M, BW), F32),
                   jax.ShapeDtypeStruct((nb, 2, NH, HD, HD), F32)],
        scratch_shapes=[pltpu.VMEM((T, BW), F32), pltpu.VMEM((2, T, BW), F32),
                        pltpu.VMEM((2, T, BW), F32), pltpu.VMEM((2, T, BW), F32),
                        pltpu.VMEM((2, NH, HD, HD), F32)],
        input_output_aliases=aliases,
        compiler_params=_cp(("parallel",)),
        name="hgrn2",
    )(*args)


def _merge_kernel(nP, x_ref, hb_ref, yfp_ref, ybp_ref, yfs_ref, ybs_ref, bonus_ref, grw_ref,
                  yhy_ref, yret_ref, yhg_ref,
                  wmg_ref, brw_ref, wout_ref, ln_ref, mod_ref, g2_ref, rt_ref, bd_ref,
                  xo_ref, h2_ref, aff_ref):
    bd = bd_ref[...]
    ln = ln_ref[...]
    is_prompt = pl.program_id(0) < nP
    halves = [jnp.where(is_prompt, yfp_ref[hh] + ybp_ref[hh], yfs_ref[hh] + ybs_ref[hh]) for hh in range(2)]
    y_sum = jnp.concatenate(halves, axis=1)
    y_rw = (_head_ln(y_sum, bd, ln[0:1], ln[1:2], RW_GN_EPS)
            + bonus_ref[...]) * grw_ref[...]
    hb = hb_ref[...]
    merged = jnp.zeros((TM, D), F32)
    for n, y in enumerate((y_rw, yhy_ref[...], yret_ref[...], yhg_ref[...])):
        gate = _sigmoid(_mm(hb, wmg_ref[:, n * D:(n + 1) * D]))
        merged = merged + gate * _mm(y.astype(BF16), brw_ref[n])
    out = _mm(merged.astype(BF16), wout_ref[...])
    mod = mod_ref[0]
    x = x_ref[...] + mod[:, 2 * D:3 * D] * out
    xo_ref[...] = x
    y = x * lax.rsqrt(jnp.mean(x * x, axis=-1, keepdims=True) + NORM_EPS) * g2_ref[...]
    h2 = y * (1.0 + mod[:, 4 * D:5 * D]) + mod[:, 3 * D:4 * D]
    h2_ref[...] = h2.astype(BF16)
    logits = _dot3(rt_ref[...], h2, _NT)
    m = jnp.max(logits, axis=0, keepdims=True)
    e = jnp.exp(logits - m)
    aff_ref[...] = e / jnp.sum(e, axis=0, keepdims=True)


def _merge(x, hb, y_p, y_s, bonus, grw, y_hy, y_ret, y_hg, w_mg, br_w, w_out, ln, mod3, g2, rt, bd, nP):
    M = x.shape[0]
    row = lambda i: (i, 0)
    const = lambda i: (0, 0)
    small = pl.BlockSpec((TM, BW), row)
    ntp = y_p[0].shape[1] // TM
    half_p = pl.BlockSpec((2, TM, 128), lambda i: (0, jnp.minimum(i, ntp - 1), 0))
    half_s = pl.BlockSpec((2, TM, 128), lambda i: (0, jnp.maximum(i - ntp, 0), 0))
    assert ntp == nP
    return pl.pallas_call(
        functools.partial(_merge_kernel, nP),
        grid=(M // TM,),
        in_specs=[pl.BlockSpec((TM, D), row), pl.BlockSpec((TM, D), row), half_p, half_p, half_s, half_s]
                 + [small] * 5
                 + [pl.BlockSpec((D, MG_COLS), const), pl.BlockSpec((4, BW, D), lambda i: (0, 0, 0)),
                    pl.BlockSpec((D, D), const), pl.BlockSpec((2, BW), const),
                    pl.BlockSpec((1, 1, 6 * D), _mod_index(nP)), pl.BlockSpec((1, D), const),
                    pl.BlockSpec((NE, D), const), pl.BlockSpec((BW, BW), const)],
        out_specs=[pl.BlockSpec((TM, D), row), pl.BlockSpec((TM, D), row),
                   pl.BlockSpec((NE, TM), lambda i: (0, i))],
        out_shape=[jax.ShapeDtypeStruct((M, D), F32), jax.ShapeDtypeStruct((M, D), BF16),
                   jax.ShapeDtypeStruct((NE, M), F32)],
        compiler_params=_cp(("parallel",)),
        name="merge",
    )(x, hb, y_p[0], y_p[1], y_s[0], y_s[1], bonus, grw, y_hy, y_ret, y_hg,
      w_mg, br_w, w_out, ln, mod3, g2, rt, bd)


def _topk_kernel(cap, aff_ref, ut_ref, *rest):
    rank_ref = rest[-1]
    aff = aff_ref[...]
    bits = pltpu.bitcast(aff, jnp.int32)
    thr = jnp.zeros((NE, 1), jnp.int32)
    for bit in range(30, -1, -1):
        cand = thr | (1 << bit)
        cnt = jnp.sum(jnp.where(bits >= cand, 1.0, 0.0), axis=1, keepdims=True)
        thr = jnp.where(cnt >= cap, cand, thr)
    gt = bits > thr
    eq = bits == thr
    need = cap - jnp.sum(jnp.where(gt, 1.0, 0.0), axis=1, keepdims=True)
    ut = ut_ref[...]
    ceq = _mm(jnp.where(eq, 1.0, 0.0).astype(BF16), ut)
    sel = jnp.logical_or(gt, jnp.logical_and(eq, ceq <= need))
    rank = _mm(jnp.where(sel, 1.0, 0.0).astype(BF16), ut) - 1.0
    rank_ref[...] = jnp.where(sel, rank, -1.0)


def _topk(affT, T, nb, cb0, ut, prev_out=None):
    M = affT.shape[1]
    in_specs = [pl.BlockSpec((NE, T), lambda b: (0, cb0 + b)), pl.BlockSpec((T, T), lambda b: (0, 0))]
    args = [affT, ut]
    aliases = {}
    if prev_out is not None:
        in_specs.append(pl.BlockSpec(memory_space=pl.ANY))
        args.append(prev_out)
        aliases = {2: 0}
    return pl.pallas_call(
        functools.partial(_topk_kernel, float(2 * T // NE)),
        grid=(nb,),
        in_specs=in_specs,
        out_specs=pl.BlockSpec((NE, T), lambda b: (0, cb0 + b)),
        out_shape=jax.ShapeDtypeStruct((NE, M), F32),
        input_output_aliases=aliases,
        compiler_params=_cp(("parallel",)),
        name="topk",
    )(*args)


def _gather_kernel(cap, rank_ref, h_ref, *rest):
    o_ref = rest[-1]
    T = h_ref.shape[0]
    slot = lax.broadcasted_iota(jnp.int32, (cap, T), 0).astype(F32)
    h = h_ref[...]
    for e in range(NE):
        p = jnp.where(rank_ref[e:e + 1, :] == slot, 1.0, 0.0).astype(BF16)
        o_ref[e] = _mm(p, h).astype(BF16)


def _gather(rank, h2b, T, nb, rb0, sb0, n_slots, prev_out=None):
    cap = 2 * T // NE
    in_specs = [pl.BlockSpec((NE, T), lambda b: (0, rb0 + b)), pl.BlockSpec((T, D), lambda b: (rb0 + b, 0))]
    args = [rank, h2b]
    aliases = {}
    if prev_out is not None:
        in_specs.append(pl.BlockSpec(memory_space=pl.ANY))
        args.append(prev_out)
        aliases = {2: 0}
    return pl.pallas_call(
        functools.partial(_gather_kernel, cap),
        grid=(nb,),
        in_specs=in_specs,
        out_specs=pl.BlockSpec((NE, cap, D), lambda b: (0, sb0 + b, 0)),
        out_shape=jax.ShapeDtypeStruct((NE, n_slots, D), BF16),
        input_output_aliases=aliases,
        compiler_params=_cp(("parallel",)),
        name="moe_gather",
    )(*args)


def _ffn_kernel(x_ref, w1_ref, w3_ref, w2_ref, o_ref):
    x = x_ref[0]
    acc = jnp.zeros(o_ref.shape[1:], F32)
    half = FF // 2
    for f in range(2):
        cols = slice(f * half, (f + 1) * half)
        h1 = _mm(x, w1_ref[0, :, cols])
        h3 = _mm(x, w3_ref[0, :, cols])
        acc = acc + _mm((_silu(h1) * h3).astype(BF16), w2_ref[0, cols, :])
    o_ref[0] = acc.astype(BF16)


def _ffn(xe, w1, w3, w2):
    n_slots = xe.shape[1]
    tm = 512
    while n_slots % tm:
        tm //= 2
    return pl.pallas_call(
        _ffn_kernel,
        grid=(NE, n_slots // tm),
        in_specs=[pl.BlockSpec((1, tm, D), lambda e, m: (e, m, 0)),
                  pl.BlockSpec((1, D, FF), lambda e, m: (e, 0, 0)),
                  pl.BlockSpec((1, D, FF), lambda e, m: (e, 0, 0)),
                  pl.BlockSpec((1, FF, D), lambda e, m: (e, 0, 0))],
        out_specs=pl.BlockSpec((1, tm, D), lambda e, m: (e, m, 0)),
        out_shape=jax.ShapeDtypeStruct((NE, n_slots, D), BF16),
        compiler_params=_cp(("parallel", "parallel")),
        name="moe_ffn",
    )(xe, w1, w3, w2)


def _combine_kernel(cap, x_ref, ye_ref, rank_ref, gate_ref, ex_ref, mod_ref, o_ref):
    tt = x_ref.shape[0]
    ex = ex_ref[...]
    rank_x = _mm(rank_ref[...].astype(BF16), ex)
    gate_x = _mm(gate_ref[...].astype(BF16), ex)
    slot = (lax.broadcasted_iota(jnp.int32, (tt, NE * cap), 1) % cap).astype(F32)
    q = jnp.where(rank_x == slot, gate_x, 0.0).astype(BF16)
    ye = ye_ref[...].reshape(NE * cap, D)
    o_ref[...] = x_ref[...] + mod_ref[0] * _mm(q, ye)


def _combine(x, ye, rank_tm, gate_tm, mod3, T, nb, rb0, sb0, nP):
    cap = 2 * T // NE
    M = x.shape[0]
    tt = TM
    nt = T // tt
    per = T // TM
    mod_idx = _mod_index(nP)
    expand = (jnp.arange(NE * cap)[None, :] // cap == jnp.arange(NE)[:, None]).astype(BF16)
    return pl.pallas_call(
        functools.partial(_combine_kernel, cap),
        grid=(nb, nt),
        in_specs=[pl.BlockSpec((tt, D), lambda b, j: ((rb0 + b) * nt + j, 0)),
                  pl.BlockSpec((NE, cap, D), lambda b, j: (0, sb0 + b, 0)),
                  pl.BlockSpec((tt, NE), lambda b, j: ((rb0 + b) * nt + j, 0)),
                  pl.BlockSpec((tt, NE), lambda b, j: ((rb0 + b) * nt + j, 0)),
                  pl.BlockSpec((NE, NE * cap), lambda b, j: (0, 0)),
                  pl.BlockSpec((1, 1, D), lambda b, j: (mod_idx((rb0 + b) * per)[0], 0, 5))],
        out_specs=pl.BlockSpec((tt, D), lambda b, j: ((rb0 + b) * nt + j, 0)),
        out_shape=jax.ShapeDtypeStruct((M, D), F32),
        input_output_aliases={0: 0},
        compiler_params=_cp(("parallel", "parallel")),
        name="moe_combine",
    )(x, ye, rank_tm, gate_tm, expand, mod3)


def _final_kernel(x_ref, g_ref, o_ref):
    x = x_ref[...]
    o_ref[...] = x * lax.rsqrt(jnp.mean(x * x, axis=-1, keepdims=True) + NORM_EPS) * g_ref[...]


def _final_norm(x, g):
    M = x.shape[0]
    return pl.pallas_call(
        _final_kernel,
        grid=(M // 512,),
        in_specs=[pl.BlockSpec((512, D), lambda i: (i, 0)), pl.BlockSpec((1, D), lambda i: (0, 0))],
        out_specs=pl.BlockSpec((512, D), lambda i: (i, 0)),
        out_shape=jax.ShapeDtypeStruct((M, D), F32),
        compiler_params=_cp(("parallel",)),
        name="final_norm",
    )(x, g)


def _rope_lane_tables(T):
    rows = T // GRID_W
    row = jnp.repeat(jnp.arange(rows, dtype=F32), GRID_W)
    col = jnp.tile(jnp.arange(GRID_W, dtype=F32), rows)
    nf = HD // 4
    inv = ROPE_BASE ** (-jnp.arange(nf, dtype=F32) / nf)
    ang = jnp.concatenate([row[:, None] * inv, col[:, None] * inv], axis=-1)
    cos, sin = jnp.cos(ang), jnp.sin(ang)
    return (jnp.tile(jnp.concatenate([cos, cos], axis=-1), (1, NH)),
            jnp.tile(jnp.concatenate([-sin, sin], axis=-1), (1, NH)))


def kernel(x_prompt, x_sample, state_rwkv, state_ret, state_hgrn, c, c_ctx, ada_w, ada_b, norm1_g, norm2_g, final_g, w_in, rw_mu, rw_w0, rw_w_up, rw_a0, rw_a_up, rw_g_up, rw_kvec, rw_ln, hy_conv, hy_ffn1, hy_ffn1_b, hy_ffn2, hy_ffn2_b, hy_ffn3, hy_freq, hy_decay, hy_skip, ret_rate, ret_gn, hg_lb, hg_norm, br_w, w_out, router, ex_w1, ex_w3, ex_w2):
    nP, nS, L = x_prompt.shape[0], x_sample.shape[0], w_in.shape[0]
    assert x_prompt.shape[1:] == (TP, D) and x_sample.shape[1:] == (TS, D)
    assert nS <= 8 and (nP * TP) % TS == 0
    MP = nP * TP
    M = MP + nS * TS
    sblk = MP // TS
    x = jnp.concatenate([x_prompt.reshape(MP, D), x_sample.reshape(nS * TS, D)], axis=0)

    cv = jnp.zeros((16, D), F32).at[:nS].set(c).at[8].set(c_ctx)
    mod = _ada(cv, ada_w, ada_b)

    lb_p = jax.nn.softmax(hg_lb.astype(F32), axis=0)
    lower_bounds = jnp.cumsum(lb_p, axis=0) - lb_p[0]

    bd = (jnp.arange(BW)[:, None] // HD == jnp.arange(BW)[None, :] // HD).astype(BF16)
    tri = _hg_tri()
    ut = {T: (jnp.arange(T)[:, None] <= jnp.arange(T)[None, :]).astype(BF16) for T in (TP, TS)}
    tabs = {T: _dft_tables(T) for T in (TP, TS)}
    feats = {T: _hy_feats(T) for T in (TP, TS)}
    cosT, sinT = _rope_lane_tables(TS)
    ones_p = jnp.ones((TP, BW), F32)
    zero_state = jnp.zeros((nP, 2, NH, HD, HD), F32)
    groups = ((TP, nP, 0, 0), (TS, nS, sblk, nP * (2 * TP // NE) // (2 * TS // NE)))
    n_slots = nP * (2 * TP // NE) + nS * (2 * TS // NE)

    rw_states, ret_states, hg_states = [], [], []
    for l in range(L):
        mod3 = mod[l].reshape(16, 1, 6 * D)
        wl = w_in[l]
        o1 = RW_COLS
        o2 = o1 + HY_COLS
        o3 = o2 + RET_COLS
        o4 = o3 + HG_COLS
        hb, z_rw, z_hy, z_ret, z_hg = _inproj(
            x, norm1_g[l][None], mod3, wl[:, :o1].astype(BF16), wl[:, o1:o2].astype(BF16),
            wl[:, o2:o3].astype(BF16), wl[:, o3:o4].astype(BF16), nP)

        wlr = jnp.zeros((128, 4 * BW), F32)
        wlr = wlr.at[0:32, 0:BW].set(rw_w_up[l, 0]).at[0:32, BW:2 * BW].set(rw_w_up[l, 1])
        wlr = wlr.at[32:64, 2 * BW:3 * BW].set(rw_a_up[l]).at[64:128, 3 * BW:].set(rw_g_up[l])
        vec = jnp.concatenate([rw_w0[l], rw_a0[l][None], rw_kvec[l], jnp.zeros((2, BW), F32)], axis=0)
        prep_p = _prep(z_rw, z_hy, rw_mu[l], hy_conv[l], wlr, vec, bd, 0, nP, TP)
        prep_s = _prep(z_rw, z_hy, rw_mu[l], hy_conv[l], wlr, vec, bd, MP // TM, nS, TS, prep_p[7:])
        bonus, g_rw, hx = prep_s[7:]

        yf_p, yb_p, s_rw_p = _rwkv_scan_group(prep_p[:7], zero_state, nP, TP)
        yf_s, yb_s, _ = _rwkv_scan_group(prep_s[:7], state_rwkv[:, l], nS, TS)
        rw_states.append(s_rw_p)

        f1p = jnp.pad(hy_ffn1[l], ((0, 128 - hy_ffn1.shape[1]), (0, 0)))
        u1 = y_hy = None
        specs = {}
        for T, nb, rb0, _ in groups:
            specs[T] = _hy_spectrum(T, feats[T], tabs[T], f1p, hy_ffn1_b[l][None], hy_ffn2[l],
                                    hy_ffn2_b[l][None], hy_ffn3[l], hy_freq[l], hy_decay[l][None])
            u1 = _hy_conv(T, nb, rb0, hx, 0, hx, 1, hy_skip[l], 0, specs[T], tabs[T], u1)
        for T, nb, rb0, _ in groups:
            y_hy = _hy_conv(T, nb, rb0, u1, 0, hx, 2, hy_skip[l], 1, specs[T], tabs[T], y_hy)

        y_ret, s_ret_p = _retention(TP, nP, 0, z_ret, ones_p, ones_p, ret_rate[l].repeat(HD, axis=-1),
                                    ret_gn[l], zero_state, bd, False)
        y_ret, _ = _retention(TS, nS, sblk, z_ret, cosT, sinT, ret_rate[l].repeat(HD, axis=-1),
                              ret_gn[l], state_ret[:, l], bd, True, y_ret)
        ret_states.append(s_ret_p)
        y_hg, s_hg_p = _hgrn(TP, nP, 0, z_hg, lower_bounds[l], hg_norm[l][None], zero_state, bd, tri)
        y_hg, _ = _hgrn(TS, nS, sblk, z_hg, lower_bounds[l], hg_norm[l][None],
                        jnp.swapaxes(state_hgrn[:, l], -1, -2), bd, tri, y_hg)
        hg_states.append(jnp.swapaxes(s_hg_p, -1, -2))

        x, h2b, affT = _merge(x, hb, (yf_p, yb_p), (yf_s, yb_s), bonus, g_rw, y_hy, y_ret, y_hg,
                              wl[:, o4:].astype(BF16), br_w[l].astype(BF16), w_out[l].astype(BF16),
                              rw_ln[l], mod3, norm2_g[l][None], router[l].T, bd, nP)

        rank = xe = None
        for T, nb, rb0, sb0 in groups:
            rank = _topk(affT, T, nb, rb0, ut[T], rank)
        for T, nb, rb0, sb0 in groups:
            xe = _gather(rank, h2b, T, nb, rb0, sb0, n_slots, xe)
        ye = _ffn(xe, ex_w1[l].astype(BF16), ex_w3[l].astype(BF16), ex_w2[l].astype(BF16))
        rank_tm, gate_tm = rank.T, affT.T
        for T, nb, rb0, sb0 in groups:
            x = _combine(x, ye, rank_tm, gate_tm, mod3, T, nb, rb0, sb0, nP)

    y = _final_norm(x, final_g[None])
    pack = lambda states: jnp.stack(states, axis=1)
    return (y[:MP].reshape(nP, TP, D), y[MP:].reshape(nS, TS, D),
            pack(rw_states), pack(ret_states), pack(hg_states))
```
